```python
import jax, jax.numpy as jnp
from jax import lax
import numpy as np

D_MODEL = 1024
BATCH = 2
SEQ = 8192
DEPTH = 1
DEC_BATCH = 2
DEC_SEQ = 16384
PAST_LEN = 128

N_META = 16
GRID_W = 64
Q_BLOCK = 128
ROPE_THETA = 10000.0
NORM_EPS = 1e-6

GQA_HEADS = 8
GQA_KV_HEADS = 2
GQA_HEAD_DIM = 64
GQA_WIDTH = GQA_HEADS * GQA_HEAD_DIM

MLA_HEADS = 8
MLA_Q_RANK = 256
MLA_KV_RANK = 128
MLA_NOPE_DIM = 64
MLA_ROPE_DIM = 32
MLA_V_DIM = 64
MLA_QK_DIM = MLA_NOPE_DIM + MLA_ROPE_DIM
MLA_WIDTH = MLA_HEADS * MLA_V_DIM

D_MIX = GQA_WIDTH + MLA_WIDTH

IN_SPLITS = (GQA_HEADS * GQA_HEAD_DIM,
             GQA_KV_HEADS * GQA_HEAD_DIM,
             GQA_KV_HEADS * GQA_HEAD_DIM,
             MLA_Q_RANK,
             MLA_KV_RANK,
             MLA_ROPE_DIM)
D_IN = 512 + 128 + 128 + 256 + 128 + 32
IN_SPLIT_POINTS = (512, 640, 768, 1024, 1152)

N_GROUPS = 4
EXPERTS_PER_GROUP = 8
N_EXPERTS = N_GROUPS * EXPERTS_PER_GROUP
TOP_K = 2
D_EXPERT = 256
MOE_BLOCK = 128

kernel_name = "hymba_gqa_mla_hier_moe_encoder"


def rms_norm(x, g):
    xf = x.astype(jnp.float32)
    y = xf * lax.rsqrt(jnp.mean(xf * xf, axis=-1, keepdims=True) + NORM_EPS)
    return (y * g.astype(jnp.float32)).astype(x.dtype)


def axial_angles(rows, cols, rot_dim):
    half = rot_dim // 2
    inv = ROPE_THETA ** (-jnp.arange(0, half, 2, dtype=jnp.float32) / half)
    ang = jnp.concatenate([rows.astype(jnp.float32)[:, None] * inv,
                           cols.astype(jnp.float32)[:, None] * inv], axis=-1)
    return jnp.cos(ang), jnp.sin(ang)


def apply_rope(x, cos, sin):
    xf = x.astype(jnp.float32).reshape(x.shape[:-1] + (x.shape[-1] // 2, 2))
    x0, x1 = xf[..., 0], xf[..., 1]
    c, s = cos[None, :, None, :], sin[None, :, None, :]
    out = jnp.stack([x0 * c - x1 * s, x0 * s + x1 * c], axis=-1)
    return out.reshape(x.shape).astype(x.dtype)


def _attend_queries(q, k, v, scale):
    s = jnp.einsum('bqhgd,bkhd->bhgqk', q, k, preferred_element_type=jnp.float32) * scale
    p = jax.nn.softmax(s, axis=-1).astype(v.dtype)
    return jnp.einsum('bhgqk,bkhe->bqhge', p, v)


def bidirectional_attention(q, k, v, scale):
    B, L, H, Dk = q.shape
    Hkv = k.shape[2]
    G = H // Hkv
    qg = q.reshape(B, L, Hkv, G, Dk)
    o_meta = _attend_queries(qg[:, :N_META], k, v, scale)
    n_blk = (L - N_META) // Q_BLOCK
    q_blk = qg[:, N_META:].reshape(B, n_blk, Q_BLOCK, Hkv, G, Dk).transpose(1, 0, 2, 3, 4, 5)
    o_blk = lax.map(lambda qb: _attend_queries(qb, k, v, scale), q_blk)
    o_real = o_blk.transpose(1, 0, 2, 3, 4, 5).reshape(B, L - N_META, Hkv, G, -1)
    return jnp.concatenate([o_meta, o_real], axis=1).reshape(B, L, H, -1)


def parallel_mixer(h, cos_g, sin_g, cos_m, sin_m, norm_mix_g, w_in, gqa_q_norm_g, gqa_k_norm_g,
                   mla_q_norm_g, mla_kv_norm_g, w_mla_q_up, w_mla_kv_up,
                   gqa_out_norm_g, mla_out_norm_g, w_out):
    B, L, _ = h.shape
    u = rms_norm(h, norm_mix_g)
    z = u @ w_in
    q_a, k_a, v_a, c_q, c_kv, k_r = jnp.split(z, list(IN_SPLIT_POINTS), axis=-1)

    q_a = apply_rope(rms_norm(q_a.reshape(B, L, GQA_HEADS, GQA_HEAD_DIM), gqa_q_norm_g), cos_g, sin_g)
    k_a = apply_rope(rms_norm(k_a.reshape(B, L, GQA_KV_HEADS, GQA_HEAD_DIM), gqa_k_norm_g), cos_g, sin_g)
    v_a = v_a.reshape(B, L, GQA_KV_HEADS, GQA_HEAD_DIM)
    o_a = bidirectional_attention(q_a, k_a, v_a, GQA_HEAD_DIM ** -0.5).reshape(B, L, GQA_WIDTH)

    q_b = (rms_norm(c_q, mla_q_norm_g) @ w_mla_q_up).reshape(B, L, MLA_HEADS, MLA_QK_DIM)
    q_nope, q_rope = q_b[..., :MLA_NOPE_DIM], q_b[..., MLA_NOPE_DIM:]
    q_rope = apply_rope(q_rope, cos_m, sin_m)
    kv = (rms_norm(c_kv, mla_kv_norm_g) @ w_mla_kv_up).reshape(B, L, MLA_HEADS, MLA_NOPE_DIM + MLA_V_DIM)
    k_nope, v_b = kv[..., :MLA_NOPE_DIM], kv[..., MLA_NOPE_DIM:]
    k_rope = apply_rope(k_r[:, :, None, :], cos_m, sin_m)
    q_full = jnp.concatenate([q_nope, q_rope], axis=-1)
    k_full = jnp.concatenate([k_nope, jnp.broadcast_to(k_rope, (B, L, MLA_HEADS, MLA_ROPE_DIM))], axis=-1)
    o_b = bidirectional_attention(q_full, k_full, v_b, MLA_QK_DIM ** -0.5).reshape(B, L, MLA_WIDTH)

    o = jnp.concatenate([rms_norm(o_a, gqa_out_norm_g), rms_norm(o_b, mla_out_norm_g)], axis=-1)
    return o @ w_out


def hierarchical_moe(u, w_rg, b_rg, w_re, b_re, w_gate, w_up, w_down):
    B, L, D = u.shape
    x = u.reshape(-1, D)
    T = x.shape[0]
    lg = (x @ w_rg).astype(jnp.float32) + b_rg.astype(jnp.float32)
    pg = jax.nn.softmax(lg, axis=-1)
    p_grp, grp = lax.top_k(pg, 1)
    le = ((x @ w_re).astype(jnp.float32) + b_re.astype(jnp.float32)).reshape(T, N_GROUPS, EXPERTS_PER_GROUP)
    le_sel = jnp.take_along_axis(le, grp[:, :, None], axis=1)[:, 0]
    top_p, top_i = lax.top_k(jax.nn.softmax(le_sel, axis=-1), TOP_K)
    gates = p_grp * (top_p / jnp.sum(top_p, axis=-1, keepdims=True))
    expert_id = grp * EXPERTS_PER_GROUP + top_i

    A = T * TOP_K
    e_flat = expert_id.reshape(-1)
    tok_flat = jnp.repeat(jnp.arange(T, dtype=jnp.int32), TOP_K)
    g_flat = gates.reshape(-1)
    order = jnp.argsort(e_flat)
    e_sorted = e_flat[order]
    counts = jnp.bincount(e_flat, length=N_EXPERTS)
    starts = jnp.cumsum(counts) - counts
    padded = (counts + MOE_BLOCK - 1) // MOE_BLOCK * MOE_BLOCK
    pends = jnp.cumsum(padded)
    pstarts = pends - padded
    slot = pstarts[e_sorted] + jnp.arange(A, dtype=counts.dtype) - starts[e_sorted]
    n_blocks = (A + N_EXPERTS * (MOE_BLOCK - 1) + MOE_BLOCK - 1) // MOE_BLOCK
    P = n_blocks * MOE_BLOCK
    buf_tok = jnp.zeros((P,), jnp.int32).at[slot].set(tok_flat[order])
    buf_gate = jnp.zeros((P,), jnp.float32).at[slot].set(g_flat[order])
    block_expert = jnp.minimum(
        jnp.searchsorted(pends, jnp.arange(n_blocks, dtype=pends.dtype) * MOE_BLOCK, side='right'),
        N_EXPERTS - 1)

    def run_block(args):
        tok, e = args
        xe = x[tok]
        hid = jax.nn.silu(xe @ w_gate[e]) * (xe @ w_up[e])
        return hid @ w_down[e]

    yb = lax.map(run_block, (buf_tok.reshape(n_blocks, MOE_BLOCK), block_expert))
    y = jnp.zeros((T, D), jnp.float32).at[buf_tok].add(yb.reshape(P, D).astype(jnp.float32) * buf_gate[:, None])
    return y.astype(u.dtype).reshape(B, L, D)


def encoder_trunk(x, meta_tokens, norm_mix_g, w_in, gqa_q_norm_g, gqa_k_norm_g, mla_q_norm_g,
                  mla_kv_norm_g, w_mla_q_up, w_mla_kv_up, gqa_out_norm_g, mla_out_norm_g, w_out,
                  norm_ffn_g, w_router_group, b_router_group, w_router_expert, b_router_expert,
                  w_expert_gate, w_expert_up, w_expert_down, norm_final_g):
    B, n, D = x.shape
    rows_count = n // GRID_W
    rows = jnp.concatenate([jnp.full((N_META,), -1, jnp.int32),
                            jnp.repeat(jnp.arange(rows_count, dtype=jnp.int32), GRID_W)])
    cols = jnp.concatenate([jnp.arange(N_META, dtype=jnp.int32),
                            jnp.tile(jnp.arange(GRID_W, dtype=jnp.int32), rows_count)])
    cos_g, sin_g = axial_angles(rows, cols, GQA_HEAD_DIM)
    cos_m, sin_m = axial_angles(rows, cols, MLA_ROPE_DIM)

    h = jnp.concatenate([jnp.broadcast_to(meta_tokens.astype(x.dtype)[None], (B, N_META, D)), x], axis=1)
    for l in range(DEPTH):
        h = h + parallel_mixer(h, cos_g, sin_g, cos_m, sin_m, norm_mix_g[l], w_in[l],
                               gqa_q_norm_g[l], gqa_k_norm_g[l], mla_q_norm_g[l], mla_kv_norm_g[l],
                               w_mla_q_up[l], w_mla_kv_up[l], gqa_out_norm_g[l], mla_out_norm_g[l], w_out[l])
        h = h + hierarchical_moe(rms_norm(h, norm_ffn_g[l]), w_router_group[l], b_router_group[l],
                                 w_router_expert[l], b_router_expert[l],
                                 w_expert_gate[l], w_expert_up[l], w_expert_down[l])
    h = rms_norm(h, norm_final_g)
    return h[:, N_META:]


def setup_inputs(seed: int = 0) -> dict:
    key = jax.random.key(seed)
    ks = jax.random.split(key, 24)
    f32 = jnp.float32

    def nrm(k, shape, scale):
        return jax.random.normal(k, shape, f32) * scale

    def gain(k, shape):
        return 1.0 + 0.02 * jax.random.normal(k, shape, f32)

    return {
        "x_prompt": jax.random.normal(ks[0], (BATCH, SEQ, D_MODEL), f32),
        "x_sample": jax.random.normal(ks[1], (DEC_BATCH, DEC_SEQ, D_MODEL), f32),
        "meta_tokens": nrm(ks[2], (N_META, D_MODEL), 1.0),
        "norm_mix_g": gain(ks[3], (DEPTH, D_MODEL)),
        "w_in": nrm(ks[4], (DEPTH, D_MODEL, D_IN), D_MODEL ** -0.5),
        "gqa_q_norm_g": gain(ks[5], (DEPTH, GQA_HEAD_DIM)),
        "gqa_k_norm_g": gain(ks[6], (DEPTH, GQA_HEAD_DIM)),
        "mla_q_norm_g": gain(ks[7], (DEPTH, MLA_Q_RANK)),
        "mla_kv_norm_g": gain(ks[8], (DEPTH, MLA_KV_RANK)),
        "w_mla_q_up": nrm(ks[9], (DEPTH, MLA_Q_RANK, MLA_HEADS * MLA_QK_DIM), MLA_Q_RANK ** -0.5),
        "w_mla_kv_up": nrm(ks[10], (DEPTH, MLA_KV_RANK, MLA_HEADS * (MLA_NOPE_DIM + MLA_V_DIM)), MLA_KV_RANK ** -0.5),
        "gqa_out_norm_g": gain(ks[11], (DEPTH, GQA_WIDTH)),
        "mla_out_norm_g": gain(ks[12], (DEPTH, MLA_WIDTH)),
        "w_out": nrm(ks[13], (DEPTH, D_MIX, D_MODEL), D_MIX ** -0.5),
        "norm_ffn_g": gain(ks[14], (DEPTH, D_MODEL)),
        "w_router_group": nrm(ks[15], (DEPTH, D_MODEL, N_GROUPS), D_MODEL ** -0.5),
        "b_router_group": nrm(ks[16], (DEPTH, N_GROUPS), 0.01),
        "w_router_expert": nrm(ks[17], (DEPTH, D_MODEL, N_EXPERTS), D_MODEL ** -0.5),
        "b_router_expert": nrm(ks[18], (DEPTH, N_EXPERTS), 0.01),
        "w_expert_gate": nrm(ks[19], (DEPTH, N_EXPERTS, D_MODEL, D_EXPERT), D_MODEL ** -0.5),
        "w_expert_up": nrm(ks[20], (DEPTH, N_EXPERTS, D_MODEL, D_EXPERT), D_MODEL ** -0.5),
        "w_expert_down": nrm(ks[21], (DEPTH, N_EXPERTS, D_EXPERT, D_MODEL), D_EXPERT ** -0.5),
        "norm_final_g": gain(ks[22], (D_MODEL,)),
    }


def reference(x_prompt, x_sample, meta_tokens, norm_mix_g, w_in, gqa_q_norm_g, gqa_k_norm_g,
              mla_q_norm_g, mla_kv_norm_g, w_mla_q_up, w_mla_kv_up, gqa_out_norm_g, mla_out_norm_g,
              w_out, norm_ffn_g, w_router_group, b_router_group, w_router_expert, b_router_expert,
              w_expert_gate, w_expert_up, w_expert_down, norm_final_g):
    y_prompt = encoder_trunk(x_prompt, meta_tokens, norm_mix_g, w_in, gqa_q_norm_g, gqa_k_norm_g,
                             mla_q_norm_g, mla_kv_norm_g, w_mla_q_up, w_mla_kv_up, gqa_out_norm_g,
                             mla_out_norm_g, w_out, norm_ffn_g, w_router_group, b_router_group,
                             w_router_expert, b_router_expert, w_expert_gate, w_expert_up,
                             w_expert_down, norm_final_g)
    y_sample = encoder_trunk(x_sample, meta_tokens, norm_mix_g, w_in, gqa_q_norm_g, gqa_k_norm_g,
                             mla_q_norm_g, mla_kv_norm_g, w_mla_q_up, w_mla_kv_up, gqa_out_norm_g,
                             mla_out_norm_g, w_out, norm_ffn_g, w_router_group, b_router_group,
                             w_router_expert, b_router_expert, w_expert_gate, w_expert_up,
                             w_expert_down, norm_final_g)
    return (y_prompt, y_sample)
```

```python
import functools
import math

import jax
import jax.numpy as jnp
from jax import lax
from jax.experimental import pallas as pl
from jax.experimental.pallas import tpu as pltpu

F32 = jnp.float32
BF16 = jnp.bfloat16

D_MODEL = 1024
N_META = 16
GRID_W = 64
ROPE_THETA = 10000.0
NORM_EPS = 1e-6

GQA_HEADS = 8
GQA_KV_HEADS = 2
GQA_HEAD_DIM = 64
GQA_WIDTH = GQA_HEADS * GQA_HEAD_DIM

MLA_HEADS = 8
MLA_Q_RANK = 256
MLA_KV_RANK = 128
MLA_NOPE_DIM = 64
MLA_ROPE_DIM = 32
MLA_V_DIM = 64
MLA_QK_DIM = MLA_NOPE_DIM + MLA_ROPE_DIM
MLA_WIDTH = MLA_HEADS * MLA_V_DIM

N_GROUPS = 4
EXPERTS_PER_GROUP = 8
N_EXPERTS = N_GROUPS * EXPERTS_PER_GROUP
D_EXPERT = 256

LANES = 128
HALF = LANES // 2
META_PAD = 128
NEG_BIG = -1e30
VMEM_LIMIT = 56 * 1024 * 1024

LOG2E = 1.4426950408889634
SCALE_A = GQA_HEAD_DIM ** -0.5 * LOG2E
SCALE_B = MLA_QK_DIM ** -0.5 * LOG2E

GQA_HEAD_ORDER = (0, 4, 1, 5, 2, 6, 3, 7)

_C_Q, _C_QS, _C_K, _C_KS, _C_V, _C_CQ, _C_CKV, _C_KR, _C_KRS, _C_END = (
    0, 512, 1024, 1152, 1280, 1408, 1664, 1792, 1920, 2048)

NT_DIMS = (((1,), (1,)), ((), ()))


def _cparams(sem):
    return pltpu.CompilerParams(dimension_semantics=sem, vmem_limit_bytes=VMEM_LIMIT)


def _full(shape):
    nd = len(shape)
    return pl.BlockSpec(shape, lambda *_: (0,) * nd)


def _rms(x, g):
    return x * lax.rsqrt(jnp.mean(x * x, axis=-1, keepdims=True) + NORM_EPS) * g


def _tile_lanes(a, reps):
    return jnp.concatenate([a] * reps, axis=1)


def _head_mean_sq(z, bd):
    sq = z * z
    hi = sq.astype(BF16)
    lo = (sq - hi.astype(F32)).astype(BF16)
    return (jnp.dot(hi, bd, preferred_element_type=F32)
            + jnp.dot(lo, bd, preferred_element_type=F32))


def _proj_kernel(x_ref, cg_ref, sg_ref, cmq_ref, smq_ref, cmk_ref, smk_ref,
                 gmix_ref, w1_ref, gq_ref, gqs_ref, gk_ref, gks_ref, gcq_ref, gckv_ref,
                 wqu_ref, wqus_ref, wku_ref, wvu_ref, bdq_ref, bdk_ref,
                 qa_ref, ka_ref, va_ref, qb_ref, kb_ref, vb_ref):
    u = _rms(x_ref[...], gmix_ref[...]).astype(BF16)
    z = jnp.dot(u, w1_ref[...], preferred_element_type=F32)

    cg, sg = cg_ref[...], sg_ref[...]
    zq, zqs = z[:, _C_Q:_C_QS], z[:, _C_QS:_C_K]
    rq = lax.rsqrt(_head_mean_sq(zq, bdq_ref[...]) + NORM_EPS) * SCALE_A
    c4, s4 = _tile_lanes(cg, 4), _tile_lanes(sg, 4)
    qa_ref[...] = (rq * (zq * gq_ref[...] * c4 + zqs * gqs_ref[...] * s4)).astype(BF16)
    zk, zks = z[:, _C_K:_C_KS], z[:, _C_KS:_C_V]
    rk = lax.rsqrt(_head_mean_sq(zk, bdk_ref[...]) + NORM_EPS)
    ka_ref[...] = (rk * (zk * gk_ref[...] * cg + zks * gks_ref[...] * sg)).astype(BF16)
    va_ref[...] = z[:, _C_V:_C_CQ].astype(BF16)

    cqn = _rms(z[:, _C_CQ:_C_CKV], gcq_ref[...]).astype(BF16)
    zqb = jnp.dot(cqn, wqu_ref[...], preferred_element_type=F32)
    zqbs = jnp.dot(cqn, wqus_ref[...], preferred_element_type=F32)
    qb_ref[...] = (zqb * _tile_lanes(cmq_ref[...], MLA_HEADS)
                   + zqbs * _tile_lanes(smq_ref[...], MLA_HEADS)).astype(BF16)
    ckvn = _rms(z[:, _C_CKV:_C_KR], gckv_ref[...]).astype(BF16)
    kn = jnp.dot(ckvn, wku_ref[...], preferred_element_type=F32)
    kr = z[:, _C_KR:_C_KRS] * cmk_ref[...] + z[:, _C_KRS:_C_END] * smk_ref[...]
    kb_ref[...] = (kn + _tile_lanes(kr, MLA_HEADS)).astype(BF16)
    vb_ref[...] = jnp.dot(ckvn, wvu_ref[...], preferred_element_type=F32).astype(BF16)


def _project(x2d, tables, pw, tm):
    T = x2d.shape[0]
    n_pos = tables[0].shape[0]
    pos_blocks = n_pos // tm
    row = lambda w: pl.BlockSpec((tm, w), lambda i: (i, 0))
    tab = pl.BlockSpec((tm, LANES), lambda i: (i % pos_blocks, 0))
    weights = (pw["gmix"], pw["w1"], pw["gq"], pw["gqs"], pw["gk"], pw["gks"], pw["gcq"],
               pw["gckv"], pw["wqu"], pw["wqus"], pw["wku"], pw["wvu"], pw["bdq"], pw["bdk"])
    widths = (GQA_WIDTH, LANES, LANES, MLA_HEADS * LANES, MLA_HEADS * LANES, MLA_WIDTH)
    return pl.pallas_call(
        _proj_kernel,
        grid=(T // tm,),
        in_specs=[row(D_MODEL)] + [tab] * 6 + [_full(w.shape) for w in weights],
        out_specs=[row(w) for w in widths],
        out_shape=[jax.ShapeDtypeStruct((T, w), BF16) for w in widths],
        compiler_params=_cparams(("parallel",)),
        name="proj",
    )(x2d, *tables, *weights)


def _flash(load_q, load_k, load_v, km, vm, bias, n_kv, tk, m_ref, l_ref, acc_ref):
    q = load_q()
    s = lax.dot_general(q, km, NT_DIMS, preferred_element_type=F32) + bias
    m0 = jnp.max(s, axis=1, keepdims=True)
    p = jnp.exp2(s - m0)
    m_ref[...] = m0
    l_ref[...] = jnp.sum(p, axis=1, keepdims=True)
    acc_ref[...] = jnp.dot(p.astype(BF16), vm, preferred_element_type=F32)

    def body(j, carry):
        off = pl.multiple_of(j * tk, tk)
        s = lax.dot_general(load_q(), load_k(off), NT_DIMS, preferred_element_type=F32)
        m_prev = m_ref[...]
        m_new = jnp.maximum(m_prev, jnp.max(s, axis=1, keepdims=True))
        alpha = jnp.exp2(m_prev - m_new)
        p = jnp.exp2(s - m_new)
        l_ref[...] = alpha * l_ref[...] + jnp.sum(p, axis=1, keepdims=True)
        acc_ref[...] = alpha * acc_ref[...] + jnp.dot(
            p.astype(BF16), load_v(off), preferred_element_type=F32)
        m_ref[...] = m_new
        return carry

    lax.fori_loop(0, n_kv, body, 0)
    return acc_ref[...] * (1.0 / l_ref[...])


def _gqa_kernel(q_ref, k_ref, v_ref, km_ref, vm_ref, bias_ref, o_ref,
                qs_ref, m_ref, l_ref, acc_ref, *, tq, tk, n_kv):
    lo = lax.broadcasted_iota(jnp.int32, (tq, LANES), 1) < HALF
    zero = jnp.zeros((tq, LANES), BF16)
    for c in range(GQA_HEADS // 2):
        qc = q_ref[0, :, c * LANES:(c + 1) * LANES]
        qs_ref[(2 * c) * tq:(2 * c + 1) * tq, :] = jnp.where(lo, qc, zero)
        qs_ref[(2 * c + 1) * tq:(2 * c + 2) * tq, :] = jnp.where(lo, zero, qc)
    o = _flash(lambda: qs_ref[...],
               lambda off: k_ref[0, pl.ds(off, tk), :],
               lambda off: v_ref[0, pl.ds(off, tk), :],
               km_ref[...], vm_ref[...], bias_ref[...], n_kv, tk, m_ref, l_ref, acc_ref)
    for c in range(GQA_HEADS // 2):
        o_ref[0, :, c * LANES:(c + 1) * LANES] = jnp.where(
            lo, o[(2 * c) * tq:(2 * c + 1) * tq], o[(2 * c + 1) * tq:(2 * c + 2) * tq])


def _gqa_attention(qa, ka, va, km, vm, bias, tq, tk):
    B, n, _ = qa.shape
    R = GQA_HEADS * tq
    kern = functools.partial(_gqa_kernel, tq=tq, tk=tk, n_kv=n // tk)
    seq = pl.BlockSpec((1, n, LANES), lambda b, i: (b, 0, 0))
    return pl.pallas_call(
        kern,
        grid=(B, n // tq),
        in_specs=[pl.BlockSpec((1, tq, GQA_WIDTH), lambda b, i: (b, i, 0)), seq, seq,
                  _full(km.shape), _full(vm.shape), _full(bias.shape)],
        out_specs=pl.BlockSpec((1, tq, GQA_WIDTH), lambda b, i: (b, i, 0)),
        out_shape=jax.ShapeDtypeStruct((B, n, GQA_WIDTH), F32),
        scratch_shapes=[pltpu.VMEM((R, LANES), BF16), pltpu.VMEM((R, 1), F32),
                        pltpu.VMEM((R, 1), F32), pltpu.VMEM((R, LANES), F32)],
        compiler_params=_cparams(("parallel", "arbitrary")),
        name="gqa_attn",
    )(qa, ka, va, km, vm, bias)


def _mla_kernel(q_ref, k_ref, v_ref, km_ref, vm_ref, bias_ref, o_ref,
                m_ref, l_ref, acc_ref, *, tq, tk, n_kv):
    outs = []
    for h in range(2):
        sl = slice(h * LANES, (h + 1) * LANES)
        outs.append(_flash(lambda: q_ref[0, :, sl],
                           lambda off: k_ref[0, pl.ds(off, tk), sl],
                           lambda off: v_ref[0, pl.ds(off, tk), :],
                           km_ref[:, sl], vm_ref[...], bias_ref[...], n_kv, tk,
                           m_ref, l_ref, acc_ref))
    lo = lax.broadcasted_iota(jnp.int32, (tq, LANES), 1) < HALF
    o_ref[0] = jnp.where(lo, outs[0], outs[1])


def _mla_attention(qb, kb, vb, km, vm, bias, tq, tk):
    B, n, _ = qb.shape
    pairs = MLA_HEADS // 2
    kern = functools.partial(_mla_kernel, tq=tq, tk=tk, n_kv=n // tk)
    return pl.pallas_call(
        kern,
        grid=(B, pairs, n // tq),
        in_specs=[pl.BlockSpec((1, tq, 2 * LANES), lambda b, j, i: (b, i, j)),
                  pl.BlockSpec((1, n, 2 * LANES), lambda b, j, i: (b, 0, j)),
                  pl.BlockSpec((1, n, LANES), lambda b, j, i: (b, 0, j)),
                  pl.BlockSpec((META_PAD, 2 * LANES), lambda b, j, i: (0, j)),
                  pl.BlockSpec((META_PAD, LANES), lambda b, j, i: (0, j)),
                  _full(bias.shape)],
        out_specs=pl.BlockSpec((1, tq, LANES), lambda b, j, i: (b, i, j)),
        out_shape=jax.ShapeDtypeStruct((B, n, MLA_WIDTH), F32),
        scratch_shapes=[pltpu.VMEM((tq, 1), F32), pltpu.VMEM((tq, 1), F32),
                        pltpu.VMEM((tq, LANES), F32)],
        compiler_params=_cparams(("parallel", "parallel", "arbitrary")),
        name="mla_attn",
    )(qb, kb, vb, km, vm, bias)


def _post_kernel(oa_ref, ob_ref, x_ref, ga_ref, gb_ref, wout_ref, gffn_ref,
                 wrh_ref, wrl_ref, br_ref, h_ref, u_ref, route_ref):
    o = jnp.concatenate([_rms(oa_ref[...], ga_ref[...]), _rms(ob_ref[...], gb_ref[...])],
                        axis=1).astype(BF16)
    h = x_ref[...] + jnp.dot(o, wout_ref[...], preferred_element_type=F32)
    h_ref[...] = h
    u = _rms(h, gffn_ref[...])
    u_ref[...] = u

    u_hi = u.astype(BF16)
    u_lo = (u - u_hi.astype(F32)).astype(BF16)
    wrh = wrh_ref[...]
    logits = (jnp.dot(u_hi, wrh, preferred_element_type=F32)
              + jnp.dot(u_lo, wrh, preferred_element_type=F32)
              + jnp.dot(u_hi, wrl_ref[...], preferred_element_type=F32)) + br_ref[...]

    lane = lax.broadcasted_iota(jnp.int32, logits.shape, 1)
    far = jnp.int32(LANES)
    neg = jnp.float32(-jnp.inf)
    lg = jnp.where(lane < N_GROUPS, logits, neg)
    mg = jnp.max(lg, axis=1, keepdims=True)
    p_grp = 1.0 / jnp.sum(jnp.exp(lg - mg), axis=1, keepdims=True)
    grp = jnp.min(jnp.where(lg == mg, lane, far), axis=1, keepdims=True)
    e_lo = N_GROUPS + grp * EXPERTS_PER_GROUP
    le = jnp.where((lane >= e_lo) & (lane < e_lo + EXPERTS_PER_GROUP), logits, neg)
    m1 = jnp.max(le, axis=1, keepdims=True)
    i1 = jnp.min(jnp.where(le == m1, lane, far), axis=1, keepdims=True)
    le2 = jnp.where(lane == i1, neg, le)
    m2 = jnp.max(le2, axis=1, keepdims=True)
    i2 = jnp.min(jnp.where(le2 == m2, lane, far), axis=1, keepdims=True)
    t = jnp.exp(m2 - m1)
    g1 = p_grp / (1.0 + t)
    g2 = g1 * t
    e1 = (i1 - N_GROUPS).astype(F32)
    e2 = (i2 - N_GROUPS).astype(F32)
    route_ref[...] = jnp.where(lane == 0, e1, jnp.where(lane == 1, e2, jnp.where(
        lane == 2, g1, jnp.where(lane == 3, g2, 0.0))))


def _post(oa, ob, x2d, pw, tm):
    T = x2d.shape[0]
    row = lambda w: pl.BlockSpec((tm, w), lambda i: (i, 0))
    weights = (pw["ga"], pw["gb"], pw["wout"], pw["gffn"], pw["wrh"], pw["wrl"], pw["br"])
    return pl.pallas_call(
        _post_kernel,
        grid=(T // tm,),
        in_specs=[row(GQA_WIDTH), row(MLA_WIDTH), row(D_MODEL)]
        + [_full(w.shape) for w in weights],
        out_specs=[row(D_MODEL), row(D_MODEL), row(LANES)],
        out_shape=[jax.ShapeDtypeStruct((T, D_MODEL), F32),
                   jax.ShapeDtypeStruct((T, D_MODEL), F32),
                   jax.ShapeDtypeStruct((T, LANES), F32)],
        compiler_params=_cparams(("parallel",)),
        name="post",
    )(oa, ob, x2d, *weights)


def _moe_kernel(be_ref, nused_ref, tok_ref, tokn_ref, dst_ref, x_hbm, wgu_ref, wd_ref,
                y_hbm, xbuf, ybuf, gsem, ssem, *, blk):
    i = pl.program_id(0)
    nb = pl.num_programs(0)
    slot = i % 2
    n_used = nused_ref[0]

    def gather_copy(t, r, s):
        return pltpu.make_async_copy(x_hbm.at[pl.ds(t, 1), :], xbuf.at[s, pl.ds(r, 1), :],
                                     gsem.at[s])

    def scatter_copy(d, r, s):
        return pltpu.make_async_copy(ybuf.at[s, pl.ds(r, 1), :], y_hbm.at[pl.ds(d, 1), :],
                                     ssem.at[s])

    def start_gather(idx_ref, s):
        def body(r, c):
            gather_copy(idx_ref[0, 0, r], r, s).start()
            return c
        lax.fori_loop(0, blk, body, 0)

    def wait_rows(make, s):
        def body(r, c):
            make(0, r, s).wait()
            return c
        lax.fori_loop(0, blk, body, 0)

    @pl.when(i == 0)
    def _():
        start_gather(tok_ref, 0)

    @pl.when(i + 1 < nb)
    def _():
        start_gather(tokn_ref, 1 - slot)

    wait_rows(gather_copy, slot)

    @pl.when(i >= 2)
    def _():
        wait_rows(scatter_copy, slot)

    @pl.when(i < n_used)
    def _():
        xe = xbuf[slot].astype(BF16)
        gu = jnp.dot(xe, wgu_ref[0], preferred_element_type=F32)
        g, up = gu[:, :D_EXPERT], gu[:, D_EXPERT:]
        hid = (g * jax.nn.sigmoid(g) * up).astype(BF16)
        ybuf[slot] = jnp.dot(hid, wd_ref[0], preferred_element_type=F32)

    @pl.when(i >= n_used)
    def _():
        ybuf[slot] = jnp.zeros((blk, D_MODEL), F32)

    def start_scatter(r, c):
        scatter_copy(dst_ref[0, 0, r], r, slot).start()
        return c
    lax.fori_loop(0, blk, start_scatter, 0)

    @pl.when(i == nb - 1)
    def _():
        @pl.when(i >= 1)
        def _():
            wait_rows(scatter_copy, 1 - slot)
        wait_rows(scatter_copy, slot)


def _moe_plan(eid, blk):
    T = eid.shape[0]
    A = 2 * T
    n_blocks = (A + N_EXPERTS * (blk - 1) + blk - 1) // blk
    P = n_blocks * blk
    e_flat = eid.T.reshape(A)
    onehot = (e_flat[:, None] == jnp.arange(N_EXPERTS, dtype=jnp.int32)[None, :]).astype(jnp.int32)
    csum = jnp.cumsum(onehot, axis=0)
    rank = jnp.sum(csum * onehot, axis=1) - 1
    counts = csum[-1]
    padded = (counts + blk - 1) // blk * blk
    pends = jnp.cumsum(padded)
    pstarts = pends - padded
    slot = jnp.sum(onehot * pstarts[None, :], axis=1) + rank
    a_idx = jnp.arange(A, dtype=jnp.int32)
    buf_tok = jnp.zeros((P,), jnp.int32).at[slot].set(a_idx % T)
    buf_a = jnp.full((P,), -1, jnp.int32).at[slot].set(a_idx)
    invalid = buf_a < 0
    buf_dst = jnp.where(invalid, A + jnp.cumsum(invalid.astype(jnp.int32)) - 1, buf_a)
    block_expert = jnp.minimum(
        jnp.searchsorted(pends, jnp.arange(n_blocks, dtype=jnp.int32) * blk, side="right"),
        N_EXPERTS - 1).astype(jnp.int32)
    n_used = (pends[-1] // blk).astype(jnp.int32).reshape(1)
    return block_expert, n_used, buf_tok, buf_dst, n_blocks


def _moe(u2d, eid, pw, blk):
    block_expert, n_used, buf_tok, buf_dst, n_blocks = _moe_plan(eid, blk)
    P = n_blocks * blk
    tok3 = buf_tok.reshape(n_blocks, 1, blk)
    dst3 = buf_dst.reshape(n_blocks, 1, blk)
    last = n_blocks - 1
    idx_spec = lambda f: pl.BlockSpec((1, 1, blk), f, memory_space=pltpu.SMEM)
    grid_spec = pltpu.PrefetchScalarGridSpec(
        num_scalar_prefetch=2,
        grid=(n_blocks,),
        in_specs=[idx_spec(lambda i, be, nu: (i, 0, 0)),
                  idx_spec(lambda i, be, nu: (jnp.minimum(i + 1, last), 0, 0)),
                  idx_spec(lambda i, be, nu: (i, 0, 0)),
                  pl.BlockSpec(memory_space=pl.ANY),
                  pl.BlockSpec((1, D_MODEL, 2 * D_EXPERT), lambda i, be, nu: (be[i], 0, 0)),
                  pl.BlockSpec((1, D_EXPERT, D_MODEL), lambda i, be, nu: (be[i], 0, 0))],
        out_specs=pl.BlockSpec(memory_space=pl.ANY),
        scratch_shapes=[pltpu.VMEM((2, blk, D_MODEL), F32), pltpu.VMEM((2, blk, D_MODEL), F32),
                        pltpu.SemaphoreType.DMA((2,)), pltpu.SemaphoreType.DMA((2,))],
    )
    return pl.pallas_call(
        functools.partial(_moe_kernel, blk=blk),
        grid_spec=grid_spec,
        out_shape=jax.ShapeDtypeStruct((P, D_MODEL), F32),
        compiler_params=_cparams(("arbitrary",)),
        name="moe",
    )(block_expert, n_used, tok3, tok3, dst3, u2d, pw["wgu"], pw["wd"])


def _final_kernel(h_ref, y1_ref, y2_ref, route_ref, g_ref, o_ref):
    r = route_ref[...]
    y = y1_ref[...] * r[:, 2:3] + y2_ref[...] * r[:, 3:4]
    o_ref[...] = _rms(h_ref[...] + y, g_ref[...])


def _final(h2d, y, route, gfin, tm):
    T = h2d.shape[0]
    second = T // tm
    return pl.pallas_call(
        _final_kernel,
        grid=(T // tm,),
        in_specs=[pl.BlockSpec((tm, D_MODEL), lambda i: (i, 0)),
                  pl.BlockSpec((tm, D_MODEL), lambda i: (i, 0)),
                  pl.BlockSpec((tm, D_MODEL), lambda i: (second + i, 0)),
                  pl.BlockSpec((tm, LANES), lambda i: (i, 0)),
                  _full(gfin.shape)],
        out_specs=pl.BlockSpec((tm, D_MODEL), lambda i: (i, 0)),
        out_shape=jax.ShapeDtypeStruct((T, D_MODEL), F32),
        compiler_params=_cparams(("parallel",)),
        name="final",
    )(h2d, y, y, route, gfin)


def _pair_swap(w):
    s = w.shape
    return w.reshape(s[:-1] + (s[-1] // 2, 2))[..., ::-1].reshape(s)


def _prep_weights(norm_mix_g, w_in, gqa_q_norm_g, gqa_k_norm_g, mla_q_norm_g, mla_kv_norm_g,
                  w_mla_q_up, w_mla_kv_up, gqa_out_norm_g, mla_out_norm_g, w_out, norm_ffn_g,
                  w_router_group, b_router_group, w_router_expert, b_router_expert,
                  w_expert_gate, w_expert_up, w_expert_down, norm_final_g):
    D = D_MODEL
    order = jnp.array(GQA_HEAD_ORDER)
    wq = w_in[:, :512].reshape(D, GQA_HEADS, GQA_HEAD_DIM)[:, order].reshape(D, GQA_WIDTH)
    wk = w_in[:, 512:640]
    wkr = w_in[:, 1152:1184]
    z64 = jnp.zeros((D, 64), F32)
    z32 = jnp.zeros((D, 32), F32)
    w1 = jnp.concatenate([
        wq, _pair_swap(wq), wk, _pair_swap(wk), w_in[:, 640:768], w_in[:, 768:1024],
        w_in[:, 1024:1152], jnp.concatenate([z64, wkr, z32], 1),
        jnp.concatenate([z64, _pair_swap(wkr), z32], 1)], axis=1).astype(BF16)

    qu = w_mla_q_up.reshape(MLA_Q_RANK, MLA_HEADS, MLA_QK_DIM)
    zq32 = jnp.zeros((MLA_Q_RANK, MLA_HEADS, 32), F32)
    zq64 = jnp.zeros((MLA_Q_RANK, MLA_HEADS, 64), F32)
    wqu = jnp.concatenate([qu, zq32], axis=2).reshape(MLA_Q_RANK, MLA_HEADS * LANES)
    wqus = jnp.concatenate([zq64, _pair_swap(qu[:, :, MLA_NOPE_DIM:]), zq32],
                           axis=2).reshape(MLA_Q_RANK, MLA_HEADS * LANES)
    kvu = w_mla_kv_up.reshape(MLA_KV_RANK, MLA_HEADS, MLA_NOPE_DIM + MLA_V_DIM)
    wku = jnp.concatenate([kvu[:, :, :MLA_NOPE_DIM],
                           jnp.zeros((MLA_KV_RANK, MLA_HEADS, 64), F32)],
                          axis=2).reshape(MLA_KV_RANK, MLA_HEADS * LANES)
    wvu = kvu[:, :, MLA_NOPE_DIM:].reshape(MLA_KV_RANK, MLA_WIDTH)

    head_block = jnp.full((GQA_HEAD_DIM, GQA_HEAD_DIM), 1.0 / GQA_HEAD_DIM, F32)
    perm_rows = lambda a: a.reshape((GQA_HEADS, GQA_HEAD_DIM) + a.shape[1:])[order].reshape(a.shape)
    wr = jnp.concatenate([w_router_group, w_router_expert,
                          jnp.zeros((D, LANES - N_GROUPS - N_EXPERTS), F32)], axis=1)
    wrh = wr.astype(BF16)
    br = jnp.concatenate([b_router_group, b_router_expert,
                          jnp.zeros((LANES - N_GROUPS - N_EXPERTS,), F32)])[None, :]
    return dict(
        gmix=norm_mix_g[None, :], w1=w1,
        gq=jnp.tile(gqa_q_norm_g, GQA_HEADS)[None, :],
        gqs=jnp.tile(_pair_swap(gqa_q_norm_g), GQA_HEADS)[None, :],
        gk=jnp.tile(gqa_k_norm_g, GQA_KV_HEADS)[None, :],
        gks=jnp.tile(_pair_swap(gqa_k_norm_g), GQA_KV_HEADS)[None, :],
        gcq=mla_q_norm_g[None, :], gckv=mla_kv_norm_g[None, :],
        wqu=wqu.astype(BF16), wqus=wqus.astype(BF16), wku=wku.astype(BF16),
        wvu=wvu.astype(BF16),
        bdq=jnp.kron(jnp.eye(GQA_HEADS, dtype=F32), head_block).astype(BF16),
        bdk=jnp.kron(jnp.eye(GQA_KV_HEADS, dtype=F32), head_block).astype(BF16),
        ga=perm_rows(gqa_out_norm_g)[None, :], gb=mla_out_norm_g[None, :],
        wout=jnp.concatenate([perm_rows(w_out[:GQA_WIDTH]), w_out[GQA_WIDTH:]], 0).astype(BF16),
        gffn=norm_ffn_g[None, :], wrh=wrh, wrl=(wr - wrh.astype(F32)).astype(BF16), br=br,
        wgu=jnp.concatenate([w_expert_gate, w_expert_up], axis=2).astype(BF16),
        wd=w_expert_down.astype(BF16), gfin=norm_final_g[None, :])


def _angles(rows, cols, rot_dim):
    half = rot_dim // 2
    inv = ROPE_THETA ** (-jnp.arange(0, half, 2, dtype=F32) / half)
    ang = jnp.concatenate([rows.astype(F32)[:, None] * inv,
                           cols.astype(F32)[:, None] * inv], axis=-1)
    return jnp.cos(ang), jnp.sin(ang)


def _rope_tables(rows, cols):
    L = rows.shape[0]
    lanes = lambda a: jnp.repeat(a, 2, axis=-1)
    sign = lambda w: jnp.tile(jnp.array([-1.0, 1.0], F32), w // 2)[None, :]
    cg, sg = _angles(rows, cols, GQA_HEAD_DIM)
    cg = jnp.tile(lanes(cg), (1, 2))
    sg = jnp.tile(lanes(sg) * sign(GQA_HEAD_DIM), (1, 2))
    cm, sm = _angles(rows, cols, MLA_ROPE_DIM)
    cm, sm = lanes(cm), lanes(sm) * sign(MLA_ROPE_DIM)
    one64, z64, z32 = jnp.ones((L, 64), F32), jnp.zeros((L, 64), F32), jnp.zeros((L, 32), F32)
    cmq = jnp.concatenate([one64, cm, z32], 1) * SCALE_B
    smq = jnp.concatenate([z64, sm, z32], 1) * SCALE_B
    cmk = jnp.concatenate([z64, cm, z32], 1)
    smk = jnp.concatenate([z64, sm, z32], 1)
    return (cg, sg, cmq, smq, cmk, smk)


def _tile_sizes(n):
    tm = min(256, n)
    tq_gqa = min(128, n)
    tq_mla = min(512, n)
    tk = min(512, n)
    return tm, tq_gqa, tq_mla, tk


MOE_ROWS = 256


def _trunk(x, pw, meta_kv, bias):
    B, n, D = x.shape
    T = B * n
    tm, tq_gqa, tq_mla, tk = _tile_sizes(n)
    t = jnp.arange(n, dtype=jnp.int32)
    tables = _rope_tables(t // GRID_W, t % GRID_W)
    x2d = x.reshape(T, D)
    qa, ka, va, qb, kb, vb = _project(x2d, tables, pw, tm)
    ka_m, va_m, kb_m, vb_m = meta_kv
    r3 = lambda a: a.reshape(B, n, a.shape[-1])
    oa = _gqa_attention(r3(qa), r3(ka), r3(va), ka_m, va_m, bias, tq_gqa, tk)
    ob = _mla_attention(r3(qb), r3(kb), r3(vb), kb_m, vb_m, bias, tq_mla, tk)
    h, u, route = _post(oa.reshape(T, GQA_WIDTH), ob.reshape(T, MLA_WIDTH), x2d, pw, tm)
    eid = route[:, :2].astype(jnp.int32)
    y = _moe(u, eid, pw, MOE_ROWS)
    out = _final(h, y, route, pw["gfin"], tm)
    return out.reshape(B, n, D)


def kernel(x_prompt, x_sample, meta_tokens, norm_mix_g, w_in, gqa_q_norm_g, gqa_k_norm_g, mla_q_norm_g, mla_kv_norm_g, w_mla_q_up, w_mla_kv_up, gqa_out_norm_g, mla_out_norm_g, w_out, norm_ffn_g, w_router_group, b_router_group, w_router_expert, b_router_expert, w_expert_gate, w_expert_up, w_expert_down, norm_final_g):
    pw = _prep_weights(norm_mix_g[0], w_in[0], gqa_q_norm_g[0], gqa_k_norm_g[0],
                       mla_q_norm_g[0], mla_kv_norm_g[0], w_mla_q_up[0], w_mla_kv_up[0],
                       gqa_out_norm_g[0], mla_out_norm_g[0], w_out[0], norm_ffn_g[0],
                       w_router_group[0], b_router_group[0], w_router_expert[0],
                       b_router_expert[0], w_expert_gate[0], w_expert_up[0],
                       w_expert_down[0], norm_final_g)
    meta_x = jnp.concatenate([meta_tokens.astype(F32),
                              jnp.zeros((META_PAD - N_META, D_MODEL), F32)], axis=0)
    mpos = jnp.arange(META_PAD, dtype=jnp.int32)
    meta_tables = _rope_tables(jnp.full((META_PAD,), -1, jnp.int32),
                               jnp.where(mpos < N_META, mpos, 0))
    _, ka_m, va_m, _, kb_m, vb_m = _project(meta_x, meta_tables, pw, META_PAD)
    bias = jnp.where(mpos < N_META, 0.0, NEG_BIG).astype(F32)[None, :]
    meta_kv = (ka_m, va_m, kb_m, vb_m)
    return (_trunk(x_prompt, pw, meta_kv, bias), _trunk(x_sample, pw, meta_kv, bias))
```

```python
import functools
import math

import jax
import jax.numpy as jnp
from jax import lax
from jax.experimental import pallas as pl
from jax.experimental.pallas import tpu as pltpu

F32 = jnp.float32
BF16 = jnp.bfloat16

D_MODEL = 1024
N_META = 16
GRID_W = 64
ROPE_THETA = 10000.0
NORM_EPS = 1e-6

GQA_HEADS = 8
GQA_KV_HEADS = 2
GQA_HEAD_DIM = 64
GQA_WIDTH = GQA_HEADS * GQA_HEAD_DIM

MLA_HEADS = 8
MLA_Q_RANK = 256
MLA_KV_RANK = 128
MLA_NOPE_DIM = 64
MLA_ROPE_DIM = 32
MLA_V_DIM = 64
MLA_QK_DIM = MLA_NOPE_DIM + MLA_ROPE_DIM
MLA_WIDTH = MLA_HEADS * MLA_V_DIM

N_GROUPS = 4
EXPERTS_PER_GROUP = 8
N_EXPERTS = N_GROUPS * EXPERTS_PER_GROUP
D_EXPERT = 256

LANES = 128
HALF = LANES // 2
META_PAD = 128
NEG_BIG = -1e30
VMEM_LIMIT = 56 * 1024 * 1024

LOG2E = 1.4426950408889634
SCALE_A = GQA_HEAD_DIM ** -0.5 * LOG2E
SCALE_B = MLA_QK_DIM ** -0.5 * LOG2E

GQA_HEAD_ORDER = (0, 4, 1, 5, 2, 6, 3, 7)

_C_Q, _C_QS, _C_K, _C_KS, _C_V, _C_CQ, _C_CKV, _C_KR, _C_KRS, _C_END = (
    0, 512, 1024, 1152, 1280, 1408, 1664, 1792, 1920, 2048)

NT_DIMS = (((1,), (1,)), ((), ()))


def _cparams(sem):
    return pltpu.CompilerParams(dimension_semantics=sem, vmem_limit_bytes=VMEM_LIMIT)


def _full(shape):
    nd = len(shape)
    return pl.BlockSpec(shape, lambda *_: (0,) * nd)


def _rms(x, g):
    return x * lax.rsqrt(jnp.mean(x * x, axis=-1, keepdims=True) + NORM_EPS) * g


def _tile_lanes(a, reps):
    return jnp.concatenate([a] * reps, axis=1)


def _head_mean_sq(z, bd):
    sq = z * z
    hi = sq.astype(BF16)
    lo = (sq - hi.astype(F32)).astype(BF16)
    return (jnp.dot(hi, bd, preferred_element_type=F32)
            + jnp.dot(lo, bd, preferred_element_type=F32))


def _proj_kernel(x_ref, cg_ref, sg_ref, cmq_ref, smq_ref, cmk_ref, smk_ref,
                 gmix_ref, w1_ref, gq_ref, gqs_ref, gk_ref, gks_ref, gcq_ref, gckv_ref,
                 wqu_ref, wqus_ref, wku_ref, wvu_ref, bdq_ref, bdk_ref,
                 qa_ref, ka_ref, va_ref, qb_ref, kb_ref, vb_ref):
    u = _rms(x_ref[...], gmix_ref[...]).astype(BF16)
    z = jnp.dot(u, w1_ref[...], preferred_element_type=F32)

    cg, sg = cg_ref[...], sg_ref[...]
    zq, zqs = z[:, _C_Q:_C_QS], z[:, _C_QS:_C_K]
    rq = lax.rsqrt(_head_mean_sq(zq, bdq_ref[...]) + NORM_EPS) * SCALE_A
    c4, s4 = _tile_lanes(cg, 4), _tile_lanes(sg, 4)
    qa_ref[...] = (rq * (zq * gq_ref[...] * c4 + zqs * gqs_ref[...] * s4)).astype(BF16)
    zk, zks = z[:, _C_K:_C_KS], z[:, _C_KS:_C_V]
    rk = lax.rsqrt(_head_mean_sq(zk, bdk_ref[...]) + NORM_EPS)
    ka_ref[...] = (rk * (zk * gk_ref[...] * cg + zks * gks_ref[...] * sg)).astype(BF16)
    va_ref[...] = z[:, _C_V:_C_CQ].astype(BF16)

    cqn = _rms(z[:, _C_CQ:_C_CKV], gcq_ref[...]).astype(BF16)
    zqb = jnp.dot(cqn, wqu_ref[...], preferred_element_type=F32)
    zqbs = jnp.dot(cqn, wqus_ref[...], preferred_element_type=F32)
    qb_ref[...] = (zqb * _tile_lanes(cmq_ref[...], MLA_HEADS)
                   + zqbs * _tile_lanes(smq_ref[...], MLA_HEADS)).astype(BF16)
    ckvn = _rms(z[:, _C_CKV:_C_KR], gckv_ref[...]).astype(BF16)
    kn = jnp.dot(ckvn, wku_ref[...], preferred_element_type=F32)
    kr = z[:, _C_KR:_C_KRS] * cmk_ref[...] + z[:, _C_KRS:_C_END] * smk_ref[...]
    kb_ref[...] = (kn + _tile_lanes(kr, MLA_HEADS)).astype(BF16)
    vb_ref[...] = jnp.dot(ckvn, wvu_ref[...], preferred_element_type=F32).astype(BF16)


def _project(x2d, tables, pw, tm):
    T = x2d.shape[0]
    n_pos = tables[0].shape[0]
    pos_blocks = n_pos // tm
    row = lambda w: pl.BlockSpec((tm, w), lambda i: (i, 0))
    tab = pl.BlockSpec((tm, LANES), lambda i: (i % pos_blocks, 0))
    weights = (pw["gmix"], pw["w1"], pw["gq"], pw["gqs"], pw["gk"], pw["gks"], pw["gcq"],
               pw["gckv"], pw["wqu"], pw["wqus"], pw["wku"], pw["wvu"], pw["bdq"], pw["bdk"])
    widths = (GQA_WIDTH, LANES, LANES, MLA_HEADS * LANES, MLA_HEADS * LANES, MLA_WIDTH)
    return pl.pallas_call(
        _proj_kernel,
        grid=(T // tm,),
        in_specs=[row(D_MODEL)] + [tab] * 6 + [_full(w.shape) for w in weights],
        out_specs=[row(w) for w in widths],
        out_shape=[jax.ShapeDtypeStruct((T, w), BF16) for w in widths],
        compiler_params=_cparams(("parallel",)),
        name="proj",
    )(x2d, *tables, *weights)


def _flash(load_q, load_k, load_vt, km, vmt, bias, n_kv, tk, acc_ref):
    s = lax.dot_general(km, load_q(), NT_DIMS, preferred_element_type=F32) + bias
    m0 = jnp.max(s, axis=0, keepdims=True)
    p = jnp.exp2(s - m0)
    l0 = jnp.sum(p, axis=0, keepdims=True)
    acc_ref[...] = jnp.dot(vmt, p.astype(BF16), preferred_element_type=F32)

    def body(j, carry):
        m_prev, l_prev = carry
        off = pl.multiple_of(j * tk, tk)
        s = lax.dot_general(load_k(off), load_q(), NT_DIMS, preferred_element_type=F32)
        m_new = jnp.maximum(m_prev, jnp.max(s, axis=0, keepdims=True))
        alpha = jnp.exp2(m_prev - m_new)
        p = jnp.exp2(s - m_new)
        l_new = alpha * l_prev + jnp.sum(p, axis=0, keepdims=True)
        acc_ref[...] = alpha * acc_ref[...] + jnp.dot(
            load_vt(off), p.astype(BF16), preferred_element_type=F32)
        return m_new, l_new

    _, l = lax.fori_loop(0, n_kv, body, (m0, l0))
    return acc_ref[...] * (1.0 / l)


def _gqa_kernel(q_ref, k_ref, vt_ref, km_ref, vmt_ref, bias_ref, o_ref,
                qs_ref, acc_ref, *, tq, tk, n_kv):
    lo = lax.broadcasted_iota(jnp.int32, (tq, LANES), 1) < HALF
    zero = jnp.zeros((tq, LANES), BF16)
    for c in range(GQA_HEADS // 2):
        qc = q_ref[0, :, c * LANES:(c + 1) * LANES]
        qs_ref[(2 * c) * tq:(2 * c + 1) * tq, :] = jnp.where(lo, qc, zero)
        qs_ref[(2 * c + 1) * tq:(2 * c + 2) * tq, :] = jnp.where(lo, zero, qc)
    ot = _flash(lambda: qs_ref[...],
                lambda off: k_ref[0, pl.ds(off, tk), :],
                lambda off: vt_ref[0, :, pl.ds(off, tk)],
                km_ref[...], vmt_ref[...], bias_ref[...], n_kv, tk, acc_ref)
    for c in range(GQA_HEADS // 2):
        pair = jnp.concatenate([ot[:HALF, (2 * c) * tq:(2 * c + 1) * tq],
                                ot[HALF:, (2 * c + 1) * tq:(2 * c + 2) * tq]], axis=0)
        o_ref[0, :, c * LANES:(c + 1) * LANES] = pair.T


def _gqa_attention(qa, ka, vat, km, vmt, bias, tq, tk):
    B, n, _ = qa.shape
    R = GQA_HEADS * tq
    kern = functools.partial(_gqa_kernel, tq=tq, tk=tk, n_kv=n // tk)
    return pl.pallas_call(
        kern,
        grid=(B, n // tq),
        in_specs=[pl.BlockSpec((1, tq, GQA_WIDTH), lambda b, i: (b, i, 0)),
                  pl.BlockSpec((1, n, LANES), lambda b, i: (b, 0, 0)),
                  pl.BlockSpec((1, LANES, n), lambda b, i: (b, 0, 0)),
                  _full(km.shape), _full(vmt.shape), _full(bias.shape)],
        out_specs=pl.BlockSpec((1, tq, GQA_WIDTH), lambda b, i: (b, i, 0)),
        out_shape=jax.ShapeDtypeStruct((B, n, GQA_WIDTH), F32),
        scratch_shapes=[pltpu.VMEM((R, LANES), BF16), pltpu.VMEM((LANES, R), F32)],
        compiler_params=_cparams(("parallel", "arbitrary")),
        name="gqa_attn",
    )(qa, ka, vat, km, vmt, bias)


def _mla_kernel(q_ref, k_ref, vt_ref, km_ref, vmt_ref, bias_ref, o_ref, acc_ref,
                *, tq, tk, n_kv):
    outs = []
    for h in range(2):
        sl = slice(h * LANES, (h + 1) * LANES)
        outs.append(_flash(lambda: q_ref[0, :, sl],
                           lambda off: k_ref[0, pl.ds(off, tk), sl],
                           lambda off: vt_ref[0, :, pl.ds(off, tk)],
                           km_ref[:, sl], vmt_ref[...], bias_ref[...], n_kv, tk, acc_ref))
    o_ref[0] = jnp.concatenate([outs[0][:HALF], outs[1][HALF:]], axis=0).T


def _mla_attention(qb, kb, vbt, km, vmt, bias, tq, tk):
    B, n, _ = qb.shape
    pairs = MLA_HEADS // 2
    kern = functools.partial(_mla_kernel, tq=tq, tk=tk, n_kv=n // tk)
    return pl.pallas_call(
        kern,
        grid=(B, pairs, n // tq),
        in_specs=[pl.BlockSpec((1, tq, 2 * LANES), lambda b, j, i: (b, i, j)),
                  pl.BlockSpec((1, n, 2 * LANES), lambda b, j, i: (b, 0, j)),
                  pl.BlockSpec((1, LANES, n), lambda b, j, i: (b, j, 0)),
                  pl.BlockSpec((META_PAD, 2 * LANES), lambda b, j, i: (0, j)),
                  pl.BlockSpec((LANES, META_PAD), lambda b, j, i: (j, 0)),
                  _full(bias.shape)],
        out_specs=pl.BlockSpec((1, tq, LANES), lambda b, j, i: (b, i, j)),
        out_shape=jax.ShapeDtypeStruct((B, n, MLA_WIDTH), F32),
        scratch_shapes=[pltpu.VMEM((LANES, tq), F32)],
        compiler_params=_cparams(("parallel", "parallel", "arbitrary")),
        name="mla_attn",
    )(qb, kb, vbt, km, vmt, bias)


def _post_kernel(oa_ref, ob_ref, x_ref, ga_ref, gb_ref, wout_ref, gffn_ref,
                 wrh_ref, wrl_ref, br_ref, h_ref, u_ref, route_ref):
    o = jnp.concatenate([_rms(oa_ref[...], ga_ref[...]), _rms(ob_ref[...], gb_ref[...])],
                        axis=1).astype(BF16)
    h = x_ref[...] + jnp.dot(o, wout_ref[...], preferred_element_type=F32)
    h_ref[...] = h
    u = _rms(h, gffn_ref[...])
    u_ref[...] = u

    u_hi = u.astype(BF16)
    u_lo = (u - u_hi.astype(F32)).astype(BF16)
    wrh = wrh_ref[...]
    logits = (jnp.dot(u_hi, wrh, preferred_element_type=F32)
              + jnp.dot(u_lo, wrh, preferred_element_type=F32)
              + jnp.dot(u_hi, wrl_ref[...], preferred_element_type=F32)) + br_ref[...]

    lane = lax.broadcasted_iota(jnp.int32, logits.shape, 1)
    far = jnp.int32(LANES)
    neg = jnp.float32(-jnp.inf)
    lg = jnp.where(lane < N_GROUPS, logits, neg)
    mg = jnp.max(lg, axis=1, keepdims=True)
    p_grp = 1.0 / jnp.sum(jnp.exp(lg - mg), axis=1, keepdims=True)
    grp = jnp.min(jnp.where(lg == mg, lane, far), axis=1, keepdims=True)
    e_lo = N_GROUPS + grp * EXPERTS_PER_GROUP
    le = jnp.where((lane >= e_lo) & (lane < e_lo + EXPERTS_PER_GROUP), logits, neg)
    m1 = jnp.max(le, axis=1, keepdims=True)
    i1 = jnp.min(jnp.where(le == m1, lane, far), axis=1, keepdims=True)
    le2 = jnp.where(lane == i1, neg, le)
    m2 = jnp.max(le2, axis=1, keepdims=True)
    i2 = jnp.min(jnp.where(le2 == m2, lane, far), axis=1, keepdims=True)
    t = jnp.exp(m2 - m1)
    g1 = p_grp / (1.0 + t)
    g2 = g1 * t
    e1 = (i1 - N_GROUPS).astype(F32)
    e2 = (i2 - N_GROUPS).astype(F32)
    route_ref[...] = jnp.where(lane == 0, e1, jnp.where(lane == 1, e2, jnp.where(
        lane == 2, g1, jnp.where(lane == 3, g2, 0.0))))


def _post(oa, ob, x2d, pw, tm):
    T = x2d.shape[0]
    row = lambda w: pl.BlockSpec((tm, w), lambda i: (i, 0))
    weights = (pw["ga"], pw["gb"], pw["wout"], pw["gffn"], pw["wrh"], pw["wrl"], pw["br"])
    return pl.pallas_call(
        _post_kernel,
        grid=(T // tm,),
        in_specs=[row(GQA_WIDTH), row(MLA_WIDTH), row(D_MODEL)]
        + [_full(w.shape) for w in weights],
        out_specs=[row(D_MODEL), row(D_MODEL), row(LANES)],
        out_shape=[jax.ShapeDtypeStruct((T, D_MODEL), F32),
                   jax.ShapeDtypeStruct((T, D_MODEL), F32),
                   jax.ShapeDtypeStruct((T, LANES), F32)],
        compiler_params=_cparams(("parallel",)),
        name="post",
    )(oa, ob, x2d, *weights)


def _moe_kernel(be_ref, nused_ref, tok_ref, tokn_ref, dst_ref, x_hbm, wgu_ref, wd_ref,
                y_hbm, xbuf, ybuf, gsem, ssem, *, blk):
    i = pl.program_id(0)
    nb = pl.num_programs(0)
    slot = i % 2
    n_used = nused_ref[0]

    def gather_copy(t, r, s):
        return pltpu.make_async_copy(x_hbm.at[pl.ds(t, 1), :], xbuf.at[s, pl.ds(r, 1), :],
                                     gsem.at[s])

    def scatter_copy(d, r, s):
        return pltpu.make_async_copy(ybuf.at[s, pl.ds(r, 1), :], y_hbm.at[pl.ds(d, 1), :],
                                     ssem.at[s])

    def start_gather(idx_ref, s):
        def body(r, c):
            gather_copy(idx_ref[0, 0, r], r, s).start()
            return c
        lax.fori_loop(0, blk, body, 0)

    def wait_rows(make, s):
        def body(r, c):
            make(0, r, s).wait()
            return c
        lax.fori_loop(0, blk, body, 0)

    @pl.when(i == 0)
    def _():
        start_gather(tok_ref, 0)

    @pl.when(i + 1 < nb)
    def _():
        start_gather(tokn_ref, 1 - slot)

    wait_rows(gather_copy, slot)

    @pl.when(i >= 2)
    def _():
        wait_rows(scatter_copy, slot)

    @pl.when(i < n_used)
    def _():
        xe = xbuf[slot].astype(BF16)
        gu = jnp.dot(xe, wgu_ref[0], preferred_element_type=F32)
        g, up = gu[:, :D_EXPERT], gu[:, D_EXPERT:]
        hid = (g * jax.nn.sigmoid(g) * up).astype(BF16)
        ybuf[slot] = jnp.dot(hid, wd_ref[0], preferred_element_type=F32)

    @pl.when(i >= n_used)
    def _():
        ybuf[slot] = jnp.zeros((blk, D_MODEL), F32)

    def start_scatter(r, c):
        scatter_copy(dst_ref[0, 0, r], r, slot).start()
        return c
    lax.fori_loop(0, blk, start_scatter, 0)

    @pl.when(i == nb - 1)
    def _():
        @pl.when(i >= 1)
        def _():
            wait_rows(scatter_copy, 1 - slot)
        wait_rows(scatter_copy, slot)


def _moe_plan(eid, blk):
    T = eid.shape[0]
    A = 2 * T
    n_blocks = (A + N_EXPERTS * (blk - 1) + blk - 1) // blk
    P = n_blocks * blk
    e_flat = eid.T.reshape(A)
    onehot = (e_flat[:, None] == jnp.arange(N_EXPERTS, dtype=jnp.int32)[None, :]).astype(jnp.int32)
    csum = jnp.cumsum(onehot, axis=0)
    rank = jnp.sum(csum * onehot, axis=1) - 1
    counts = csum[-1]
    padded = (counts + blk - 1) // blk * blk
    pends = jnp.cumsum(padded)
    pstarts = pends - padded
    slot = jnp.sum(onehot * pstarts[None, :], axis=1) + rank
    a_idx = jnp.arange(A, dtype=jnp.int32)
    buf_tok = jnp.zeros((P,), jnp.int32).at[slot].set(a_idx % T)
    buf_a = jnp.full((P,), -1, jnp.int32).at[slot].set(a_idx)
    invalid = buf_a < 0
    buf_dst = jnp.where(invalid, A + jnp.cumsum(invalid.astype(jnp.int32)) - 1, buf_a)
    block_expert = jnp.minimum(
        jnp.searchsorted(pends, jnp.arange(n_blocks, dtype=jnp.int32) * blk, side="right"),
        N_EXPERTS - 1).astype(jnp.int32)
    n_used = (pends[-1] // blk).astype(jnp.int32).reshape(1)
    return block_expert, n_used, buf_tok, buf_dst, n_blocks


def _moe(u2d, eid, pw, blk):
    block_expert, n_used, buf_tok, buf_dst, n_blocks = _moe_plan(eid, blk)
    P = n_blocks * blk
    tok3 = buf_tok.reshape(n_blocks, 1, blk)
    dst3 = buf_dst.reshape(n_blocks, 1, blk)
    last = n_blocks - 1
    idx_spec = lambda f: pl.BlockSpec((1, 1, blk), f, memory_space=pltpu.SMEM)
    grid_spec = pltpu.PrefetchScalarGridSpec(
        num_scalar_prefetch=2,
        grid=(n_blocks,),
        in_specs=[idx_spec(lambda i, be, nu: (i, 0, 0)),
                  idx_spec(lambda i, be, nu: (jnp.minimum(i + 1, last), 0, 0)),
                  idx_spec(lambda i, be, nu: (i, 0, 0)),
                  pl.BlockSpec(memory_space=pl.ANY),
                  pl.BlockSpec((1, D_MODEL, 2 * D_EXPERT), lambda i, be, nu: (be[i], 0, 0)),
                  pl.BlockSpec((1, D_EXPERT, D_MODEL), lambda i, be, nu: (be[i], 0, 0))],
        out_specs=pl.BlockSpec(memory_space=pl.ANY),
        scratch_shapes=[pltpu.VMEM((2, blk, D_MODEL), F32), pltpu.VMEM((2, blk, D_MODEL), F32),
                        pltpu.SemaphoreType.DMA((2,)), pltpu.SemaphoreType.DMA((2,))],
    )
    return pl.pallas_call(
        functools.partial(_moe_kernel, blk=blk),
        grid_spec=grid_spec,
        out_shape=jax.ShapeDtypeStruct((P, D_MODEL), F32),
        compiler_params=_cparams(("arbitrary",)),
        name="moe",
    )(block_expert, n_used, tok3, tok3, dst3, u2d, pw["wgu"], pw["wd"])


def _final_kernel(h_ref, y1_ref, y2_ref, route_ref, g_ref, o_ref):
    r = route_ref[...]
    y = y1_ref[...] * r[:, 2:3] + y2_ref[...] * r[:, 3:4]
    o_ref[...] = _rms(h_ref[...] + y, g_ref[...])


def _final(h2d, y, route, gfin, tm):
    T = h2d.shape[0]
    second = T // tm
    return pl.pallas_call(
        _final_kernel,
        grid=(T // tm,),
        in_specs=[pl.BlockSpec((tm, D_MODEL), lambda i: (i, 0)),
                  pl.BlockSpec((tm, D_MODEL), lambda i: (i, 0)),
                  pl.BlockSpec((tm, D_MODEL), lambda i: (second + i, 0)),
                  pl.BlockSpec((tm, LANES), lambda i: (i, 0)),
                  _full(gfin.shape)],
        out_specs=pl.BlockSpec((tm, D_MODEL), lambda i: (i, 0)),
        out_shape=jax.ShapeDtypeStruct((T, D_MODEL), F32),
        compiler_params=_cparams(("parallel",)),
        name="final",
    )(h2d, y, y, route, gfin)


def _pair_swap(w):
    s = w.shape
    return w.reshape(s[:-1] + (s[-1] // 2, 2))[..., ::-1].reshape(s)


def _prep_weights(norm_mix_g, w_in, gqa_q_norm_g, gqa_k_norm_g, mla_q_norm_g, mla_kv_norm_g,
                  w_mla_q_up, w_mla_kv_up, gqa_out_norm_g, mla_out_norm_g, w_out, norm_ffn_g,
                  w_router_group, b_router_group, w_router_expert, b_router_expert,
                  w_expert_gate, w_expert_up, w_expert_down, norm_final_g):
    D = D_MODEL
    order = jnp.array(GQA_HEAD_ORDER)
    wq = w_in[:, :512].reshape(D, GQA_HEADS, GQA_HEAD_DIM)[:, order].reshape(D, GQA_WIDTH)
    wk = w_in[:, 512:640]
    wkr = w_in[:, 1152:1184]
    z64 = jnp.zeros((D, 64), F32)
    z32 = jnp.zeros((D, 32), F32)
    w1 = jnp.concatenate([
        wq, _pair_swap(wq), wk, _pair_swap(wk), w_in[:, 640:768], w_in[:, 768:1024],
        w_in[:, 1024:1152], jnp.concatenate([z64, wkr, z32], 1),
        jnp.concatenate([z64, _pair_swap(wkr), z32], 1)], axis=1).astype(BF16)

    qu = w_mla_q_up.reshape(MLA_Q_RANK, MLA_HEADS, MLA_QK_DIM)
    zq32 = jnp.zeros((MLA_Q_RANK, MLA_HEADS, 32), F32)
    zq64 = jnp.zeros((MLA_Q_RANK, MLA_HEADS, 64), F32)
    wqu = jnp.concatenate([qu, zq32], axis=2).reshape(MLA_Q_RANK, MLA_HEADS * LANES)
    wqus = jnp.concatenate([zq64, _pair_swap(qu[:, :, MLA_NOPE_DIM:]), zq32],
                           axis=2).reshape(MLA_Q_RANK, MLA_HEADS * LANES)
    kvu = w_mla_kv_up.reshape(MLA_KV_RANK, MLA_HEADS, MLA_NOPE_DIM + MLA_V_DIM)
    wku = jnp.concatenate([kvu[:, :, :MLA_NOPE_DIM],
                           jnp.zeros((MLA_KV_RANK, MLA_HEADS, 64), F32)],
                          axis=2).reshape(MLA_KV_RANK, MLA_HEADS * LANES)
    wvu = kvu[:, :, MLA_NOPE_DIM:].reshape(MLA_KV_RANK, MLA_WIDTH)

    head_block = jnp.full((GQA_HEAD_DIM, GQA_HEAD_DIM), 1.0 / GQA_HEAD_DIM, F32)
    perm_rows = lambda a: a.reshape((GQA_HEADS, GQA_HEAD_DIM) + a.shape[1:])[order].reshape(a.shape)
    wr = jnp.concatenate([w_router_group, w_router_expert,
                          jnp.zeros((D, LANES - N_GROUPS - N_EXPERTS), F32)], axis=1)
    wrh = wr.astype(BF16)
    br = jnp.concatenate([b_router_group, b_router_expert,
                          jnp.zeros((LANES - N_GROUPS - N_EXPERTS,), F32)])[None, :]
    return dict(
        gmix=norm_mix_g[None, :], w1=w1,
        gq=jnp.tile(gqa_q_norm_g, GQA_HEADS)[None, :],
        gqs=jnp.tile(_pair_swap(gqa_q_norm_g), GQA_HEADS)[None, :],
        gk=jnp.tile(gqa_k_norm_g, GQA_KV_HEADS)[None, :],
        gks=jnp.tile(_pair_swap(gqa_k_norm_g), GQA_KV_HEADS)[None, :],
        gcq=mla_q_norm_g[None, :], gckv=mla_kv_norm_g[None, :],
        wqu=wqu.astype(BF16), wqus=wqus.astype(BF16), wku=wku.astype(BF16),
        wvu=wvu.astype(BF16),
        bdq=jnp.kron(jnp.eye(GQA_HEADS, dtype=F32), head_block).astype(BF16),
        bdk=jnp.kron(jnp.eye(GQA_KV_HEADS, dtype=F32), head_block).astype(BF16),
        ga=perm_rows(gqa_out_norm_g)[None, :], gb=mla_out_norm_g[None, :],
        wout=jnp.concatenate([perm_rows(w_out[:GQA_WIDTH]), w_out[GQA_WIDTH:]], 0).astype(BF16),
        gffn=norm_ffn_g[None, :], wrh=wrh, wrl=(wr - wrh.astype(F32)).astype(BF16), br=br,
        wgu=jnp.concatenate([w_expert_gate, w_expert_up], axis=2).astype(BF16),
        wd=w_expert_down.astype(BF16), gfin=norm_final_g[None, :])


def _angles(rows, cols, rot_dim):
    half = rot_dim // 2
    inv = ROPE_THETA ** (-jnp.arange(0, half, 2, dtype=F32) / half)
    ang = jnp.concatenate([rows.astype(F32)[:, None] * inv,
                           cols.astype(F32)[:, None] * inv], axis=-1)
    return jnp.cos(ang), jnp.sin(ang)


def _rope_tables(rows, cols):
    L = rows.shape[0]
    lanes = lambda a: jnp.repeat(a, 2, axis=-1)
    sign = lambda w: jnp.tile(jnp.array([-1.0, 1.0], F32), w // 2)[None, :]
    cg, sg = _angles(rows, cols, GQA_HEAD_DIM)
    cg = jnp.tile(lanes(cg), (1, 2))
    sg = jnp.tile(lanes(sg) * sign(GQA_HEAD_DIM), (1, 2))
    cm, sm = _angles(rows, cols, MLA_ROPE_DIM)
    cm, sm = lanes(cm), lanes(sm) * sign(MLA_ROPE_DIM)
    one64, z64, z32 = jnp.ones((L, 64), F32), jnp.zeros((L, 64), F32), jnp.zeros((L, 32), F32)
    cmq = jnp.concatenate([one64, cm, z32], 1) * SCALE_B
    smq = jnp.concatenate([z64, sm, z32], 1) * SCALE_B
    cmk = jnp.concatenate([z64, cm, z32], 1)
    smk = jnp.concatenate([z64, sm, z32], 1)
    return (cg, sg, cmq, smq, cmk, smk)


def _tile_sizes(n):
    tm = min(256, n)
    tq_gqa = min(128, n)
    tq_mla = min(1024, n)
    tk = min(512, n)
    return tm, tq_gqa, tq_mla, tk


MOE_ROWS = 256


def _trunk(x, pw, meta_kv, bias):
    B, n, D = x.shape
    T = B * n
    tm, tq_gqa, tq_mla, tk = _tile_sizes(n)
    t = jnp.arange(n, dtype=jnp.int32)
    tables = _rope_tables(t // GRID_W, t % GRID_W)
    x2d = x.reshape(T, D)
    qa, ka, va, qb, kb, vb = _project(x2d, tables, pw, tm)
    ka_m, vat_m, kb_m, vbt_m = meta_kv
    r3 = lambda a: a.reshape(B, n, a.shape[-1])
    t3 = lambda a: jnp.swapaxes(r3(a), 1, 2)
    oa = _gqa_attention(r3(qa), r3(ka), t3(va), ka_m, vat_m, bias, tq_gqa, tk)
    ob = _mla_attention(r3(qb), r3(kb), t3(vb), kb_m, vbt_m, bias, tq_mla, tk)
    h, u, route = _post(oa.reshape(T, GQA_WIDTH), ob.reshape(T, MLA_WIDTH), x2d, pw, tm)
    eid = route[:, :2].astype(jnp.int32)
    y = _moe(u, eid, pw, MOE_ROWS)
    out = _final(h, y, route, pw["gfin"], tm)
    return out.reshape(B, n, D)


def kernel(x_prompt, x_sample, meta_tokens, norm_mix_g, w_in, gqa_q_norm_g, gqa_k_norm_g, mla_q_norm_g, mla_kv_norm_g, w_mla_q_up, w_mla_kv_up, gqa_out_norm_g, mla_out_norm_g, w_out, norm_ffn_g, w_router_group, b_router_group, w_router_expert, b_router_expert, w_expert_gate, w_expert_up, w_expert_down, norm_final_g):
    pw = _prep_weights(norm_mix_g[0], w_in[0], gqa_q_norm_g[0], gqa_k_norm_g[0],
                       mla_q_norm_g[0], mla_kv_norm_g[0], w_mla_q_up[0], w_mla_kv_up[0],
                       gqa_out_norm_g[0], mla_out_norm_g[0], w_out[0], norm_ffn_g[0],
                       w_router_group[0], b_router_group[0], w_router_expert[0],
                       b_router_expert[0], w_expert_gate[0], w_expert_up[0],
                       w_expert_down[0], norm_final_g)
    meta_x = jnp.concatenate([meta_tokens.astype(F32),
                              jnp.zeros((META_PAD - N_META, D_MODEL), F32)], axis=0)
    mpos = jnp.arange(META_PAD, dtype=jnp.int32)
    meta_tables = _rope_tables(jnp.full((META_PAD,), -1, jnp.int32),
                               jnp.where(mpos < N_META, mpos, 0))
    _, ka_m, va_m, _, kb_m, vb_m = _project(meta_x, meta_tables, pw, META_PAD)
    bias = jnp.where(mpos < N_META, 0.0, NEG_BIG).astype(F32)[:, None]
    meta_kv = (ka_m, va_m.T, kb_m, vb_m.T)
    return (_trunk(x_prompt, pw, meta_kv, bias), _trunk(x_sample, pw, meta_kv, bias))
```

```python
import functools
import math

import jax
import jax.numpy as jnp
from jax import lax
from jax.experimental import pallas as pl
from jax.experimental.pallas import tpu as pltpu

F32 = jnp.float32
BF16 = jnp.bfloat16

D_MODEL = 1024
N_META = 16
GRID_W = 64
ROPE_THETA = 10000.0
NORM_EPS = 1e-6

GQA_HEADS = 8
GQA_KV_HEADS = 2
GQA_HEAD_DIM = 64
GQA_WIDTH = GQA_HEADS * GQA_HEAD_DIM

MLA_HEADS = 8
MLA_Q_RANK = 256
MLA_KV_RANK = 128
MLA_NOPE_DIM = 64
MLA_ROPE_DIM = 32
MLA_V_DIM = 64
MLA_QK_DIM = MLA_NOPE_DIM + MLA_ROPE_DIM
MLA_WIDTH = MLA_HEADS * MLA_V_DIM

N_GROUPS = 4
EXPERTS_PER_GROUP = 8
N_EXPERTS = N_GROUPS * EXPERTS_PER_GROUP
D_EXPERT = 256

LANES = 128
HALF = LANES // 2
META_PAD = 128
NEG_BIG = -1e30
VMEM_LIMIT = 56 * 1024 * 1024

LOG2E = 1.4426950408889634
SCALE_A = GQA_HEAD_DIM ** -0.5 * LOG2E
SCALE_B = MLA_QK_DIM ** -0.5 * LOG2E

GQA_HEAD_ORDER = (0, 4, 1, 5, 2, 6, 3, 7)

_C_Q, _C_QS, _C_K, _C_KS, _C_V, _C_CQ, _C_CKV, _C_KR, _C_KRS, _C_END = (
    0, 512, 1024, 1152, 1280, 1408, 1664, 1792, 1920, 2048)

NT_DIMS = (((1,), (1,)), ((), ()))


def _cparams(sem):
    return pltpu.CompilerParams(dimension_semantics=sem, vmem_limit_bytes=VMEM_LIMIT)


def _full(shape):
    nd = len(shape)
    return pl.BlockSpec(shape, lambda *_: (0,) * nd)


def _rms(x, g):
    return x * lax.rsqrt(jnp.mean(x * x, axis=-1, keepdims=True) + NORM_EPS) * g


def _tile_lanes(a, reps):
    return jnp.concatenate([a] * reps, axis=1)


def _head_mean_sq(z, bd):
    sq = z * z
    hi = sq.astype(BF16)
    lo = (sq - hi.astype(F32)).astype(BF16)
    return (jnp.dot(hi, bd, preferred_element_type=F32)
            + jnp.dot(lo, bd, preferred_element_type=F32))


def _proj_kernel(x_ref, cg_ref, sg_ref, cmq_ref, smq_ref, cmk_ref, smk_ref,
                 gmix_ref, w1_ref, gq_ref, gqs_ref, gk_ref, gks_ref, gcq_ref, gckv_ref,
                 wqu_ref, wqus_ref, wku_ref, wvu_ref, bdq_ref, bdk_ref,
                 qa_ref, ka_ref, va_ref, qb_ref, kb_ref, vb_ref):
    u = _rms(x_ref[...], gmix_ref[...]).astype(BF16)
    z = jnp.dot(u, w1_ref[...], preferred_element_type=F32)

    cg, sg = cg_ref[...], sg_ref[...]
    zq, zqs = z[:, _C_Q:_C_QS], z[:, _C_QS:_C_K]
    rq = lax.rsqrt(_head_mean_sq(zq, bdq_ref[...]) + NORM_EPS) * SCALE_A
    c4, s4 = _tile_lanes(cg, 4), _tile_lanes(sg, 4)
    qa_ref[...] = (rq * (zq * gq_ref[...] * c4 + zqs * gqs_ref[...] * s4)).astype(BF16)
    zk, zks = z[:, _C_K:_C_KS], z[:, _C_KS:_C_V]
    rk = lax.rsqrt(_head_mean_sq(zk, bdk_ref[...]) + NORM_EPS)
    ka_ref[...] = (rk * (zk * gk_ref[...] * cg + zks * gks_ref[...] * sg)).astype(BF16)
    va_ref[...] = z[:, _C_V:_C_CQ].astype(BF16)

    cqn = _rms(z[:, _C_CQ:_C_CKV], gcq_ref[...]).astype(BF16)
    zqb = jnp.dot(cqn, wqu_ref[...], preferred_element_type=F32)
    zqbs = jnp.dot(cqn, wqus_ref[...], preferred_element_type=F32)
    qb_ref[...] = (zqb * _tile_lanes(cmq_ref[...], MLA_HEADS)
                   + zqbs * _tile_lanes(smq_ref[...], MLA_HEADS)).astype(BF16)
    ckvn = _rms(z[:, _C_CKV:_C_KR], gckv_ref[...]).astype(BF16)
    kn = jnp.dot(ckvn, wku_ref[...], preferred_element_type=F32)
    kr = z[:, _C_KR:_C_KRS] * cmk_ref[...] + z[:, _C_KRS:_C_END] * smk_ref[...]
    kb_ref[...] = (kn + _tile_lanes(kr, MLA_HEADS)).astype(BF16)
    vb_ref[...] = jnp.dot(ckvn, wvu_ref[...], preferred_element_type=F32).astype(BF16)


def _project(x2d, tables, pw, tm):
    T = x2d.shape[0]
    n_pos = tables[0].shape[0]
    pos_blocks = n_pos // tm
    row = lambda w: pl.BlockSpec((tm, w), lambda i: (i, 0))
    tab = pl.BlockSpec((tm, LANES), lambda i: (i % pos_blocks, 0))
    weights = (pw["gmix"], pw["w1"], pw["gq"], pw["gqs"], pw["gk"], pw["gks"], pw["gcq"],
               pw["gckv"], pw["wqu"], pw["wqus"], pw["wku"], pw["wvu"], pw["bdq"], pw["bdk"])
    widths = (GQA_WIDTH, LANES, LANES, MLA_HEADS * LANES, MLA_HEADS * LANES, MLA_WIDTH)
    return pl.pallas_call(
        _proj_kernel,
        grid=(T // tm,),
        in_specs=[row(D_MODEL)] + [tab] * 6 + [_full(w.shape) for w in weights],
        out_specs=[row(w) for w in widths],
        out_shape=[jax.ShapeDtypeStruct((T, w), BF16) for w in widths],
        compiler_params=_cparams(("parallel",)),
        name="proj",
    )(x2d, *tables, *weights)


def _flash(load_q, load_k, load_vt, km, vmt, bias, n_kv, tk, acc_ref, s0_ref, s1_ref):
    assert n_kv >= 2 and n_kv % 2 == 0

    def scores(j):
        off = pl.multiple_of(j * tk, tk)
        return lax.dot_general(load_k(off), load_q(), NT_DIMS, preferred_element_type=F32)

    def consume(s, j, m_prev, l_prev):
        off = pl.multiple_of(j * tk, tk)
        m_new = jnp.maximum(m_prev, jnp.max(s, axis=0, keepdims=True))
        alpha = jnp.exp2(m_prev - m_new)
        p = jnp.exp2(s - m_new)
        l_new = alpha * l_prev + jnp.sum(p, axis=0, keepdims=True)
        acc_ref[...] = alpha * acc_ref[...] + jnp.dot(
            load_vt(off), p.astype(BF16), preferred_element_type=F32)
        return m_new, l_new

    s = lax.dot_general(km, load_q(), NT_DIMS, preferred_element_type=F32) + bias
    m0 = jnp.max(s, axis=0, keepdims=True)
    p = jnp.exp2(s - m0)
    l0 = jnp.sum(p, axis=0, keepdims=True)
    acc_ref[...] = jnp.dot(vmt, p.astype(BF16), preferred_element_type=F32)
    s0_ref[...] = scores(0)

    def pair(i, carry, last):
        m, l = carry
        j = 2 * i
        s1_ref[...] = scores(j + 1)
        m, l = consume(s0_ref[...], j, m, l)
        if not last:
            s0_ref[...] = scores(j + 2)
        return consume(s1_ref[...], j + 1, m, l)

    carry = lax.fori_loop(0, n_kv // 2 - 1, lambda i, c: pair(i, c, False), (m0, l0))
    _, l = pair(n_kv // 2 - 1, carry, True)
    return acc_ref[...] * (1.0 / l)


def _gqa_kernel(q_ref, k_ref, vt_ref, km_ref, vmt_ref, bias_ref, o_ref,
                qs_ref, acc_ref, s0_ref, s1_ref, *, tq, tk, n_kv):
    lo = lax.broadcasted_iota(jnp.int32, (tq, LANES), 1) < HALF
    zero = jnp.zeros((tq, LANES), BF16)
    for c in range(GQA_HEADS // 2):
        qc = q_ref[0, :, c * LANES:(c + 1) * LANES]
        qs_ref[(2 * c) * tq:(2 * c + 1) * tq, :] = jnp.where(lo, qc, zero)
        qs_ref[(2 * c + 1) * tq:(2 * c + 2) * tq, :] = jnp.where(lo, zero, qc)
    ot = _flash(lambda: qs_ref[...],
                lambda off: k_ref[0, pl.ds(off, tk), :],
                lambda off: vt_ref[0, :, pl.ds(off, tk)],
                km_ref[...], vmt_ref[...], bias_ref[...], n_kv, tk, acc_ref, s0_ref, s1_ref)
    for c in range(GQA_HEADS // 2):
        pair = jnp.concatenate([ot[:HALF, (2 * c) * tq:(2 * c + 1) * tq],
                                ot[HALF:, (2 * c + 1) * tq:(2 * c + 2) * tq]], axis=0)
        o_ref[0, :, c * LANES:(c + 1) * LANES] = pair.T


def _gqa_attention(qa, ka, vat, km, vmt, bias, tq, tk):
    B, n, _ = qa.shape
    R = GQA_HEADS * tq
    kern = functools.partial(_gqa_kernel, tq=tq, tk=tk, n_kv=n // tk)
    return pl.pallas_call(
        kern,
        grid=(B, n // tq),
        in_specs=[pl.BlockSpec((1, tq, GQA_WIDTH), lambda b, i: (b, i, 0)),
                  pl.BlockSpec((1, n, LANES), lambda b, i: (b, 0, 0)),
                  pl.BlockSpec((1, LANES, n), lambda b, i: (b, 0, 0)),
                  _full(km.shape), _full(vmt.shape), _full(bias.shape)],
        out_specs=pl.BlockSpec((1, tq, GQA_WIDTH), lambda b, i: (b, i, 0)),
        out_shape=jax.ShapeDtypeStruct((B, n, GQA_WIDTH), F32),
        scratch_shapes=[pltpu.VMEM((R, LANES), BF16), pltpu.VMEM((LANES, R), F32),
                        pltpu.VMEM((tk, R), F32), pltpu.VMEM((tk, R), F32)],
        compiler_params=_cparams(("parallel", "arbitrary")),
        name="gqa_attn",
    )(qa, ka, vat, km, vmt, bias)


def _mla_kernel(q_ref, k_ref, vt_ref, km_ref, vmt_ref, bias_ref, o_ref, acc_ref,
                s0_ref, s1_ref, *, tq, tk, n_kv):
    outs = []
    for h in range(2):
        sl = slice(h * LANES, (h + 1) * LANES)
        outs.append(_flash(lambda: q_ref[0, :, sl],
                           lambda off: k_ref[0, pl.ds(off, tk), sl],
                           lambda off: vt_ref[0, :, pl.ds(off, tk)],
                           km_ref[:, sl], vmt_ref[...], bias_ref[...], n_kv, tk,
                           acc_ref, s0_ref, s1_ref))
    o_ref[0] = jnp.concatenate([outs[0][:HALF], outs[1][HALF:]], axis=0).T


def _mla_attention(qb, kb, vbt, km, vmt, bias, tq, tk):
    B, n, _ = qb.shape
    pairs = MLA_HEADS // 2
    kern = functools.partial(_mla_kernel, tq=tq, tk=tk, n_kv=n // tk)
    return pl.pallas_call(
        kern,
        grid=(B, pairs, n // tq),
        in_specs=[pl.BlockSpec((1, tq, 2 * LANES), lambda b, j, i: (b, i, j)),
                  pl.BlockSpec((1, n, 2 * LANES), lambda b, j, i: (b, 0, j)),
                  pl.BlockSpec((1, LANES, n), lambda b, j, i: (b, j, 0)),
                  pl.BlockSpec((META_PAD, 2 * LANES), lambda b, j, i: (0, j)),
                  pl.BlockSpec((LANES, META_PAD), lambda b, j, i: (j, 0)),
                  _full(bias.shape)],
        out_specs=pl.BlockSpec((1, tq, LANES), lambda b, j, i: (b, i, j)),
        out_shape=jax.ShapeDtypeStruct((B, n, MLA_WIDTH), F32),
        scratch_shapes=[pltpu.VMEM((LANES, tq), F32), pltpu.VMEM((tk, tq), F32),
                        pltpu.VMEM((tk, tq), F32)],
        compiler_params=_cparams(("parallel", "parallel", "arbitrary")),
        name="mla_attn",
    )(qb, kb, vbt, km, vmt, bias)


def _post_kernel(oa_ref, ob_ref, x_ref, ga_ref, gb_ref, wout_ref, gffn_ref,
                 wrh_ref, wrl_ref, br_ref, h_ref, u_ref, route_ref):
    o = jnp.concatenate([_rms(oa_ref[...], ga_ref[...]), _rms(ob_ref[...], gb_ref[...])],
                        axis=1).astype(BF16)
    h = x_ref[...] + jnp.dot(o, wout_ref[...], preferred_element_type=F32)
    h_ref[...] = h
    u = _rms(h, gffn_ref[...])
    u_ref[...] = u

    u_hi = u.astype(BF16)
    u_lo = (u - u_hi.astype(F32)).astype(BF16)
    wrh = wrh_ref[...]
    logits = (jnp.dot(u_hi, wrh, preferred_element_type=F32)
              + jnp.dot(u_lo, wrh, preferred_element_type=F32)
              + jnp.dot(u_hi, wrl_ref[...], preferred_element_type=F32)) + br_ref[...]

    lane = lax.broadcasted_iota(jnp.int32, logits.shape, 1)
    far = jnp.int32(LANES)
    neg = jnp.float32(-jnp.inf)
    lg = jnp.where(lane < N_GROUPS, logits, neg)
    mg = jnp.max(lg, axis=1, keepdims=True)
    p_grp = 1.0 / jnp.sum(jnp.exp(lg - mg), axis=1, keepdims=True)
    grp = jnp.min(jnp.where(lg == mg, lane, far), axis=1, keepdims=True)
    e_lo = N_GROUPS + grp * EXPERTS_PER_GROUP
    le = jnp.where((lane >= e_lo) & (lane < e_lo + EXPERTS_PER_GROUP), logits, neg)
    m1 = jnp.max(le, axis=1, keepdims=True)
    i1 = jnp.min(jnp.where(le == m1, lane, far), axis=1, keepdims=True)
    le2 = jnp.where(lane == i1, neg, le)
    m2 = jnp.max(le2, axis=1, keepdims=True)
    i2 = jnp.min(jnp.where(le2 == m2, lane, far), axis=1, keepdims=True)
    t = jnp.exp(m2 - m1)
    g1 = p_grp / (1.0 + t)
    g2 = g1 * t
    e1 = (i1 - N_GROUPS).astype(F32)
    e2 = (i2 - N_GROUPS).astype(F32)
    route_ref[...] = jnp.where(lane == 0, e1, jnp.where(lane == 1, e2, jnp.where(
        lane == 2, g1, jnp.where(lane == 3, g2, 0.0))))


def _post(oa, ob, x2d, pw, tm):
    T = x2d.shape[0]
    row = lambda w: pl.BlockSpec((tm, w), lambda i: (i, 0))
    weights = (pw["ga"], pw["gb"], pw["wout"], pw["gffn"], pw["wrh"], pw["wrl"], pw["br"])
    return pl.pallas_call(
        _post_kernel,
        grid=(T // tm,),
        in_specs=[row(GQA_WIDTH), row(MLA_WIDTH), row(D_MODEL)]
        + [_full(w.shape) for w in weights],
        out_specs=[row(D_MODEL), row(D_MODEL), row(LANES)],
        out_shape=[jax.ShapeDtypeStruct((T, D_MODEL), F32),
                   jax.ShapeDtypeStruct((T, D_MODEL), F32),
                   jax.ShapeDtypeStruct((T, LANES), F32)],
        compiler_params=_cparams(("parallel",)),
        name="post",
    )(oa, ob, x2d, *weights)


def _moe_kernel(be_ref, nused_ref, tok_ref, tokn_ref, dstp_ref, dst_ref, x_hbm, wgu_ref, wd_ref,
                y_hbm, xbuf, ybuf, gsem, ssem, *, blk):
    i = pl.program_id(0)
    nb = pl.num_programs(0)
    slot = i % 2
    n_used = nused_ref[0]

    def gather_copy(t, r, s):
        return pltpu.make_async_copy(x_hbm.at[pl.ds(t, 1), :], xbuf.at[s, pl.ds(r, 1), :],
                                     gsem.at[s])

    def scatter_copy(d, r, s):
        return pltpu.make_async_copy(ybuf.at[s, pl.ds(r, 1), :], y_hbm.at[pl.ds(d, 1), :],
                                     ssem.at[s])

    def start_rows(make, idx_ref, s, unrolled):
        if unrolled:
            for r in range(blk):
                make(idx_ref[0, 0, r], r, s).start()
        else:
            def body(r, c):
                make(idx_ref[0, 0, r], r, s).start()
                return c
            lax.fori_loop(0, blk, body, 0)

    def wait_rows(make, s):
        def body(r, c):
            make(0, r, s).wait()
            return c
        lax.fori_loop(0, blk, body, 0, unroll=8)

    def experts():
        xe = xbuf[slot].astype(BF16)
        gu = jnp.dot(xe, wgu_ref[0], preferred_element_type=F32)
        g, up = gu[:, :D_EXPERT], gu[:, D_EXPERT:]
        hid = (g * jax.nn.sigmoid(g) * up).astype(BF16)
        ybuf[slot] = jnp.dot(hid, wd_ref[0], preferred_element_type=F32)

    @pl.when(i == 0)
    def _():
        start_rows(gather_copy, tok_ref, 0, False)

    wait_rows(gather_copy, slot)

    @pl.when(i >= 2)
    def _():
        wait_rows(scatter_copy, slot)

    @pl.when(i == 0)
    def _():
        start_rows(gather_copy, tokn_ref, 1 - slot, False)
        experts()

    @pl.when((i >= 1) & (i < n_used))
    def _():
        start_rows(gather_copy, tokn_ref, 1 - slot, True)
        experts()
        start_rows(scatter_copy, dstp_ref, 1 - slot, True)

    @pl.when((i >= 1) & (i >= n_used))
    def _():
        start_rows(gather_copy, tokn_ref, 1 - slot, False)
        ybuf[slot] = jnp.zeros((blk, D_MODEL), F32)
        start_rows(scatter_copy, dstp_ref, 1 - slot, False)

    @pl.when(i == nb - 1)
    def _():
        start_rows(scatter_copy, dst_ref, slot, False)
        wait_rows(gather_copy, 1 - slot)

        @pl.when(i >= 1)
        def _():
            wait_rows(scatter_copy, 1 - slot)
        wait_rows(scatter_copy, slot)


def _moe_plan(eid, blk):
    T = eid.shape[0]
    A = 2 * T
    n_blocks = (A + N_EXPERTS * (blk - 1) + blk - 1) // blk
    P = n_blocks * blk
    e_flat = eid.T.reshape(A)
    onehot = (e_flat[:, None] == jnp.arange(N_EXPERTS, dtype=jnp.int32)[None, :]).astype(jnp.int32)
    csum = jnp.cumsum(onehot, axis=0)
    rank = jnp.sum(csum * onehot, axis=1) - 1
    counts = csum[-1]
    padded = (counts + blk - 1) // blk * blk
    pends = jnp.cumsum(padded)
    pstarts = pends - padded
    slot = jnp.sum(onehot * pstarts[None, :], axis=1) + rank
    a_idx = jnp.arange(A, dtype=jnp.int32)
    buf_a = jnp.full((P,), -1, jnp.int32).at[slot].set(a_idx)
    invalid = buf_a < 0
    buf_tok = jnp.where(invalid, 0, buf_a % T)
    buf_dst = jnp.where(invalid, A + jnp.cumsum(invalid.astype(jnp.int32)) - 1, buf_a)
    block_expert = jnp.minimum(
        jnp.searchsorted(pends, jnp.arange(n_blocks, dtype=jnp.int32) * blk, side="right"),
        N_EXPERTS - 1).astype(jnp.int32)
    n_used = (pends[-1] // blk).astype(jnp.int32).reshape(1)
    return block_expert, n_used, buf_tok, buf_dst, n_blocks


def _moe(u2d, eid, pw, blk):
    block_expert, n_used, buf_tok, buf_dst, n_blocks = _moe_plan(eid, blk)
    P = n_blocks * blk
    tok3 = buf_tok.reshape(n_blocks, 1, blk)
    dst3 = buf_dst.reshape(n_blocks, 1, blk)
    last = n_blocks - 1
    idx_spec = lambda f: pl.BlockSpec((1, 1, blk), f, memory_space=pltpu.SMEM)
    grid_spec = pltpu.PrefetchScalarGridSpec(
        num_scalar_prefetch=2,
        grid=(n_blocks,),
        in_specs=[idx_spec(lambda i, be, nu: (i, 0, 0)),
                  idx_spec(lambda i, be, nu: (jnp.minimum(i + 1, last), 0, 0)),
                  idx_spec(lambda i, be, nu: (jnp.maximum(i - 1, 0), 0, 0)),
                  idx_spec(lambda i, be, nu: (i, 0, 0)),
                  pl.BlockSpec(memory_space=pl.ANY),
                  pl.BlockSpec((1, D_MODEL, 2 * D_EXPERT), lambda i, be, nu: (be[i], 0, 0)),
                  pl.BlockSpec((1, D_EXPERT, D_MODEL), lambda i, be, nu: (be[i], 0, 0))],
        out_specs=pl.BlockSpec(memory_space=pl.ANY),
        scratch_shapes=[pltpu.VMEM((2, blk, D_MODEL), F32), pltpu.VMEM((2, blk, D_MODEL), F32),
                        pltpu.SemaphoreType.DMA((2,)), pltpu.SemaphoreType.DMA((2,))],
    )
    return pl.pallas_call(
        functools.partial(_moe_kernel, blk=blk),
        grid_spec=grid_spec,
        out_shape=jax.ShapeDtypeStruct((P, D_MODEL), F32),
        compiler_params=_cparams(("arbitrary",)),
        name="moe",
    )(block_expert, n_used, tok3, tok3, dst3, dst3, u2d, pw["wgu"], pw["wd"])


def _final_kernel(h_ref, y1_ref, y2_ref, route_ref, g_ref, o_ref):
    r = route_ref[...]
    y = y1_ref[...] * r[:, 2:3] + y2_ref[...] * r[:, 3:4]
    o_ref[...] = _rms(h_ref[...] + y, g_ref[...])


def _final(h2d, y, route, gfin, tm):
    T = h2d.shape[0]
    second = T // tm
    return pl.pallas_call(
        _final_kernel,
        grid=(T // tm,),
        in_specs=[pl.BlockSpec((tm, D_MODEL), lambda i: (i, 0)),
                  pl.BlockSpec((tm, D_MODEL), lambda i: (i, 0)),
                  pl.BlockSpec((tm, D_MODEL), lambda i: (second + i, 0)),
                  pl.BlockSpec((tm, LANES), lambda i: (i, 0)),
                  _full(gfin.shape)],
        out_specs=pl.BlockSpec((tm, D_MODEL), lambda i: (i, 0)),
        out_shape=jax.ShapeDtypeStruct((T, D_MODEL), F32),
        compiler_params=_cparams(("parallel",)),
        name="final",
    )(h2d, y, y, route, gfin)


def _pair_swap(w):
    s = w.shape
    return w.reshape(s[:-1] + (s[-1] // 2, 2))[..., ::-1].reshape(s)


def _prep_weights(norm_mix_g, w_in, gqa_q_norm_g, gqa_k_norm_g, mla_q_norm_g, mla_kv_norm_g,
                  w_mla_q_up, w_mla_kv_up, gqa_out_norm_g, mla_out_norm_g, w_out, norm_ffn_g,
                  w_router_group, b_router_group, w_router_expert, b_router_expert,
                  w_expert_gate, w_expert_up, w_expert_down, norm_final_g):
    D = D_MODEL
    order = jnp.array(GQA_HEAD_ORDER)
    wq = w_in[:, :512].reshape(D, GQA_HEADS, GQA_HEAD_DIM)[:, order].reshape(D, GQA_WIDTH)
    wk = w_in[:, 512:640]
    wkr = w_in[:, 1152:1184]
    z64 = jnp.zeros((D, 64), F32)
    z32 = jnp.zeros((D, 32), F32)
    w1 = jnp.concatenate([
        wq, _pair_swap(wq), wk, _pair_swap(wk), w_in[:, 640:768], w_in[:, 768:1024],
        w_in[:, 1024:1152], jnp.concatenate([z64, wkr, z32], 1),
        jnp.concatenate([z64, _pair_swap(wkr), z32], 1)], axis=1).astype(BF16)

    qu = w_mla_q_up.reshape(MLA_Q_RANK, MLA_HEADS, MLA_QK_DIM)
    zq32 = jnp.zeros((MLA_Q_RANK, MLA_HEADS, 32), F32)
    zq64 = jnp.zeros((MLA_Q_RANK, MLA_HEADS, 64), F32)
    wqu = jnp.concatenate([qu, zq32], axis=2).reshape(MLA_Q_RANK, MLA_HEADS * LANES)
    wqus = jnp.concatenate([zq64, _pair_swap(qu[:, :, MLA_NOPE_DIM:]), zq32],
                           axis=2).reshape(MLA_Q_RANK, MLA_HEADS * LANES)
    kvu = w_mla_kv_up.reshape(MLA_KV_RANK, MLA_HEADS, MLA_NOPE_DIM + MLA_V_DIM)
    wku = jnp.concatenate([kvu[:, :, :MLA_NOPE_DIM],
                           jnp.zeros((MLA_KV_RANK, MLA_HEADS, 64), F32)],
                          axis=2).reshape(MLA_KV_RANK, MLA_HEADS * LANES)
    wvu = kvu[:, :, MLA_NOPE_DIM:].reshape(MLA_KV_RANK, MLA_WIDTH)

    head_block = jnp.full((GQA_HEAD_DIM, GQA_HEAD_DIM), 1.0 / GQA_HEAD_DIM, F32)
    perm_rows = lambda a: a.reshape((GQA_HEADS, GQA_HEAD_DIM) + a.shape[1:])[order].reshape(a.shape)
    wr = jnp.concatenate([w_router_group, w_router_expert,
                          jnp.zeros((D, LANES - N_GROUPS - N_EXPERTS), F32)], axis=1)
    wrh = wr.astype(BF16)
    br = jnp.concatenate([b_router_group, b_router_expert,
                          jnp.zeros((LANES - N_GROUPS - N_EXPERTS,), F32)])[None, :]
    return dict(
        gmix=norm_mix_g[None, :], w1=w1,
        gq=jnp.tile(gqa_q_norm_g, GQA_HEADS)[None, :],
        gqs=jnp.tile(_pair_swap(gqa_q_norm_g), GQA_HEADS)[None, :],
        gk=jnp.tile(gqa_k_norm_g, GQA_KV_HEADS)[None, :],
        gks=jnp.tile(_pair_swap(gqa_k_norm_g), GQA_KV_HEADS)[None, :],
        gcq=mla_q_norm_g[None, :], gckv=mla_kv_norm_g[None, :],
        wqu=wqu.astype(BF16), wqus=wqus.astype(BF16), wku=wku.astype(BF16),
        wvu=wvu.astype(BF16),
        bdq=jnp.kron(jnp.eye(GQA_HEADS, dtype=F32), head_block).astype(BF16),
        bdk=jnp.kron(jnp.eye(GQA_KV_HEADS, dtype=F32), head_block).astype(BF16),
        ga=perm_rows(gqa_out_norm_g)[None, :], gb=mla_out_norm_g[None, :],
        wout=jnp.concatenate([perm_rows(w_out[:GQA_WIDTH]), w_out[GQA_WIDTH:]], 0).astype(BF16),
        gffn=norm_ffn_g[None, :], wrh=wrh, wrl=(wr - wrh.astype(F32)).astype(BF16), br=br,
        wgu=jnp.concatenate([w_expert_gate, w_expert_up], axis=2).astype(BF16),
        wd=w_expert_down.astype(BF16), gfin=norm_final_g[None, :])


def _angles(rows, cols, rot_dim):
    half = rot_dim // 2
    inv = ROPE_THETA ** (-jnp.arange(0, half, 2, dtype=F32) / half)
    ang = jnp.concatenate([rows.astype(F32)[:, None] * inv,
                           cols.astype(F32)[:, None] * inv], axis=-1)
    return jnp.cos(ang), jnp.sin(ang)


def _rope_tables(rows, cols):
    L = rows.shape[0]
    lanes = lambda a: jnp.repeat(a, 2, axis=-1)
    sign = lambda w: jnp.tile(jnp.array([-1.0, 1.0], F32), w // 2)[None, :]
    cg, sg = _angles(rows, cols, GQA_HEAD_DIM)
    cg = jnp.tile(lanes(cg), (1, 2))
    sg = jnp.tile(lanes(sg) * sign(GQA_HEAD_DIM), (1, 2))
    cm, sm = _angles(rows, cols, MLA_ROPE_DIM)
    cm, sm = lanes(cm), lanes(sm) * sign(MLA_ROPE_DIM)
    one64, z64, z32 = jnp.ones((L, 64), F32), jnp.zeros((L, 64), F32), jnp.zeros((L, 32), F32)
    cmq = jnp.concatenate([one64, cm, z32], 1) * SCALE_B
    smq = jnp.concatenate([z64, sm, z32], 1) * SCALE_B
    cmk = jnp.concatenate([z64, cm, z32], 1)
    smk = jnp.concatenate([z64, sm, z32], 1)
    return (cg, sg, cmq, smq, cmk, smk)


def _tile_sizes(n):
    tm = min(256, n)
    tq_gqa = min(128, n)
    tq_mla = min(1024, n)
    tk = min(512, n // 2)
    return tm, tq_gqa, tq_mla, tk


MOE_ROWS = 256


def _trunk(x, pw, meta_kv, bias):
    B, n, D = x.shape
    T = B * n
    tm, tq_gqa, tq_mla, tk = _tile_sizes(n)
    t = jnp.arange(n, dtype=jnp.int32)
    tables = _rope_tables(t // GRID_W, t % GRID_W)
    x2d = x.reshape(T, D)
    qa, ka, va, qb, kb, vb = _project(x2d, tables, pw, tm)
    ka_m, vat_m, kb_m, vbt_m = meta_kv
    r3 = lambda a: a.reshape(B, n, a.shape[-1])
    t3 = lambda a: jnp.swapaxes(r3(a), 1, 2)
    oa = _gqa_attention(r3(qa), r3(ka), t3(va), ka_m, vat_m, bias, tq_gqa, tk)
    ob = _mla_attention(r3(qb), r3(kb), t3(vb), kb_m, vbt_m, bias, tq_mla, tk)
    h, u, route = _post(oa.reshape(T, GQA_WIDTH), ob.reshape(T, MLA_WIDTH), x2d, pw, tm)
    eid = route[:, :2].astype(jnp.int32)
    y = _moe(u, eid, pw, MOE_ROWS)
    out = _final(h, y, route, pw["gfin"], tm)
    return out.reshape(B, n, D)


def kernel(x_prompt, x_sample, meta_tokens, norm_mix_g, w_in, gqa_q_norm_g, gqa_k_norm_g, mla_q_norm_g, mla_kv_norm_g, w_mla_q_up, w_mla_kv_up, gqa_out_norm_g, mla_out_norm_g, w_out, norm_ffn_g, w_router_group, b_router_group, w_router_expert, b_router_expert, w_expert_gate, w_expert_up, w_expert_down, norm_final_g):
    pw = _prep_weights(norm_mix_g[0], w_in[0], gqa_q_norm_g[0], gqa_k_norm_g[0],
                       mla_q_norm_g[0], mla_kv_norm_g[0], w_mla_q_up[0], w_mla_kv_up[0],
                       gqa_out_norm_g[0], mla_out_norm_g[0], w_out[0], norm_ffn_g[0],
                       w_router_group[0], b_router_group[0], w_router_expert[0],
                       b_router_expert[0], w_expert_gate[0], w_expert_up[0],
                       w_expert_down[0], norm_final_g)
    meta_x = jnp.concatenate([meta_tokens.astype(F32),
                              jnp.zeros((META_PAD - N_META, D_MODEL), F32)], axis=0)
    mpos = jnp.arange(META_PAD, dtype=jnp.int32)
    meta_tables = _rope_tables(jnp.full((META_PAD,), -1, jnp.int32),
                               jnp.where(mpos < N_META, mpos, 0))
    _, ka_m, va_m, _, kb_m, vb_m = _project(meta_x, meta_tables, pw, META_PAD)
    bias = jnp.where(mpos < N_META, 0.0, NEG_BIG).astype(F32)[:, None]
    meta_kv = (ka_m, va_m.T, kb_m, vb_m.T)
    return (_trunk(x_prompt, pw, meta_kv, bias), _trunk(x_sample, pw, meta_kv, bias))
```

```python
import functools
import math

import jax
import jax.numpy as jnp
from jax import lax
from jax.experimental import pallas as pl
from jax.experimental.pallas import tpu as pltpu

F32 = jnp.float32
BF16 = jnp.bfloat16

D_MODEL = 1024
N_META = 16
GRID_W = 64
ROPE_THETA = 10000.0
NORM_EPS = 1e-6

GQA_HEADS = 8
GQA_KV_HEADS = 2
GQA_HEAD_DIM = 64
GQA_WIDTH = GQA_HEADS * GQA_HEAD_DIM

MLA_HEADS = 8
MLA_Q_RANK = 256
MLA_KV_RANK = 128
MLA_NOPE_DIM = 64
MLA_ROPE_DIM = 32
MLA_V_DIM = 64
MLA_QK_DIM = MLA_NOPE_DIM + MLA_ROPE_DIM
MLA_WIDTH = MLA_HEADS * MLA_V_DIM

N_GROUPS = 4
EXPERTS_PER_GROUP = 8
N_EXPERTS = N_GROUPS * EXPERTS_PER_GROUP
D_EXPERT = 256

LANES = 128
HALF = LANES // 2
META_PAD = 128
ONES_ROWS = 16
MLA_VROWS = MLA_V_DIM + ONES_ROWS
KV_UNROLL = 8
NEG_BIG = -1e30
VMEM_LIMIT = 56 * 1024 * 1024

LOG2E = 1.4426950408889634
SCALE_A = GQA_HEAD_DIM ** -0.5 * LOG2E
SCALE_B = MLA_QK_DIM ** -0.5 * LOG2E

GQA_HEAD_ORDER = (0, 4, 1, 5, 2, 6, 3, 7)

_C_Q, _C_QS, _C_K, _C_KS, _C_V, _C_CQ, _C_CKV, _C_KR, _C_KRS, _C_END = (
    0, 512, 1024, 1152, 1280, 1408, 1664, 1792, 1920, 2048)

NT_DIMS = (((1,), (1,)), ((), ()))


def _cparams(sem):
    return pltpu.CompilerParams(dimension_semantics=sem, vmem_limit_bytes=VMEM_LIMIT)


def _full(shape):
    nd = len(shape)
    return pl.BlockSpec(shape, lambda *_: (0,) * nd)


def _rms(x, g):
    return x * lax.rsqrt(jnp.mean(x * x, axis=-1, keepdims=True) + NORM_EPS) * g


def _tile_lanes(a, reps):
    return jnp.concatenate([a] * reps, axis=1)


def _head_mean_sq(z, bd):
    sq = z * z
    hi = sq.astype(BF16)
    lo = (sq - hi.astype(F32)).astype(BF16)
    return (jnp.dot(hi, bd, preferred_element_type=F32)
            + jnp.dot(lo, bd, preferred_element_type=F32))


def _proj_kernel(x_ref, cg_ref, sg_ref, cmq_ref, smq_ref, cmk_ref, smk_ref,
                 gmix_ref, w1_ref, gq_ref, gqs_ref, gk_ref, gks_ref, gcq_ref, gckv_ref,
                 wqu_ref, wqus_ref, wku_ref, wvu_ref, bdq_ref, bdk_ref,
                 qa_ref, ka_ref, va_ref, qb_ref, kb_ref, vb_ref):
    u = _rms(x_ref[...], gmix_ref[...]).astype(BF16)
    z = jnp.dot(u, w1_ref[...], preferred_element_type=F32)

    cg, sg = cg_ref[...], sg_ref[...]
    zq, zqs = z[:, _C_Q:_C_QS], z[:, _C_QS:_C_K]
    rq = lax.rsqrt(_head_mean_sq(zq, bdq_ref[...]) + NORM_EPS) * SCALE_A
    c4, s4 = _tile_lanes(cg, 4), _tile_lanes(sg, 4)
    qa_ref[...] = (rq * (zq * gq_ref[...] * c4 + zqs * gqs_ref[...] * s4)).astype(BF16)
    zk, zks = z[:, _C_K:_C_KS], z[:, _C_KS:_C_V]
    rk = lax.rsqrt(_head_mean_sq(zk, bdk_ref[...]) + NORM_EPS)
    ka_ref[...] = (rk * (zk * gk_ref[...] * cg + zks * gks_ref[...] * sg)).astype(BF16)
    va_ref[...] = z[:, _C_V:_C_CQ].astype(BF16)

    cqn = _rms(z[:, _C_CQ:_C_CKV], gcq_ref[...]).astype(BF16)
    zqb = jnp.dot(cqn, wqu_ref[...], preferred_element_type=F32)
    zqbs = jnp.dot(cqn, wqus_ref[...], preferred_element_type=F32)
    qb_ref[...] = (zqb * _tile_lanes(cmq_ref[...], MLA_HEADS)
                   + zqbs * _tile_lanes(smq_ref[...], MLA_HEADS)).astype(BF16)
    ckvn = _rms(z[:, _C_CKV:_C_KR], gckv_ref[...]).astype(BF16)
    kn = jnp.dot(ckvn, wku_ref[...], preferred_element_type=F32)
    kr = z[:, _C_KR:_C_KRS] * cmk_ref[...] + z[:, _C_KRS:_C_END] * smk_ref[...]
    kb_ref[...] = (kn + _tile_lanes(kr, MLA_HEADS)).astype(BF16)
    vb_ref[...] = jnp.dot(ckvn, wvu_ref[...], preferred_element_type=F32).astype(BF16)


def _project(x2d, tables, pw, tm):
    T = x2d.shape[0]
    n_pos = tables[0].shape[0]
    pos_blocks = n_pos // tm
    row = lambda w: pl.BlockSpec((tm, w), lambda i: (i, 0))
    tab = pl.BlockSpec((tm, LANES), lambda i: (i % pos_blocks, 0))
    weights = (pw["gmix"], pw["w1"], pw["gq"], pw["gqs"], pw["gk"], pw["gks"], pw["gcq"],
               pw["gckv"], pw["wqu"], pw["wqus"], pw["wku"], pw["wvu"], pw["bdq"], pw["bdk"])
    widths = (GQA_WIDTH, LANES, LANES, MLA_HEADS * LANES, MLA_HEADS * LANES, MLA_WIDTH)
    return pl.pallas_call(
        _proj_kernel,
        grid=(T // tm,),
        in_specs=[row(D_MODEL)] + [tab] * 6 + [_full(w.shape) for w in weights],
        out_specs=[row(w) for w in widths],
        out_shape=[jax.ShapeDtypeStruct((T, w), BF16) for w in widths],
        compiler_params=_cparams(("parallel",)),
        name="proj",
    )(x2d, *tables, *weights)


def _flash(load_q, load_k, load_vt, km, vmt, bias, n_kv, tk, vdim, acc_ref, s_refs):
    assert n_kv >= KV_UNROLL and n_kv % KV_UNROLL == 0 and KV_UNROLL % 2 == 0

    def scores(j):
        off = pl.multiple_of(j * tk, tk)
        return lax.dot_general(load_k(off), load_q(), NT_DIMS, preferred_element_type=F32)

    def consume(s, j, m_prev):
        off = pl.multiple_of(j * tk, tk)
        m_new = jnp.maximum(m_prev, jnp.max(s, axis=0, keepdims=True))
        alpha = jnp.exp2(m_prev - m_new)
        p = jnp.exp2(s - m_new)
        acc_ref[...] = alpha * acc_ref[...] + jnp.dot(
            load_vt(off), p.astype(BF16), preferred_element_type=F32)
        return m_new

    s = lax.dot_general(km, load_q(), NT_DIMS, preferred_element_type=F32) + bias
    m0 = jnp.max(s, axis=0, keepdims=True)
    acc_ref[...] = jnp.dot(vmt, jnp.exp2(s - m0).astype(BF16), preferred_element_type=F32)
    s_refs[0][...] = scores(0)

    def group(i, m, last):
        j0 = KV_UNROLL * i
        for u in range(KV_UNROLL):
            if not (last and u == KV_UNROLL - 1):
                s_refs[(u + 1) % 2][...] = scores(j0 + u + 1)
            m = consume(s_refs[u % 2][...], j0 + u, m)
        return m

    trips = n_kv // KV_UNROLL
    m = lax.fori_loop(0, trips - 1, lambda i, c: group(i, c, False), m0)
    group(trips - 1, m, True)
    acc = acc_ref[...]
    return acc[:vdim] * (1.0 / acc[vdim:vdim + 1])


def _gqa_kernel(q_ref, k_ref, vt_ref, km_ref, vmt_ref, bias_ref, o_ref,
                qs_ref, acc_ref, s0_ref, s1_ref, *, tq, tk, n_kv):
    lo = lax.broadcasted_iota(jnp.int32, (tq, LANES), 1) < HALF
    zero = jnp.zeros((tq, LANES), BF16)
    for c in range(GQA_HEADS // 2):
        qc = q_ref[0, :, c * LANES:(c + 1) * LANES]
        qs_ref[(2 * c) * tq:(2 * c + 1) * tq, :] = jnp.where(lo, qc, zero)
        qs_ref[(2 * c + 1) * tq:(2 * c + 2) * tq, :] = jnp.where(lo, zero, qc)
    ot = _flash(lambda: qs_ref[...],
                lambda off: k_ref[0, pl.ds(off, tk), :],
                lambda off: vt_ref[0, :, pl.ds(off, tk)],
                km_ref[...], vmt_ref[...], bias_ref[...], n_kv, tk, LANES,
                acc_ref, (s0_ref, s1_ref))
    for c in range(GQA_HEADS // 2):
        pair = jnp.concatenate([ot[:HALF, (2 * c) * tq:(2 * c + 1) * tq],
                                ot[HALF:, (2 * c + 1) * tq:(2 * c + 2) * tq]], axis=0)
        o_ref[0, :, c * LANES:(c + 1) * LANES] = pair.T


def _gqa_attention(qa, ka, vat, km, vmt, bias, tq, tk):
    B, n, _ = qa.shape
    R = GQA_HEADS * tq
    kern = functools.partial(_gqa_kernel, tq=tq, tk=tk, n_kv=n // tk)
    return pl.pallas_call(
        kern,
        grid=(B, n // tq),
        in_specs=[pl.BlockSpec((1, tq, GQA_WIDTH), lambda b, i: (b, i, 0)),
                  pl.BlockSpec((1, n, LANES), lambda b, i: (b, 0, 0)),
                  pl.BlockSpec((1, LANES + ONES_ROWS, n), lambda b, i: (b, 0, 0)),
                  _full(km.shape), _full(vmt.shape), _full(bias.shape)],
        out_specs=pl.BlockSpec((1, tq, GQA_WIDTH), lambda b, i: (b, i, 0)),
        out_shape=jax.ShapeDtypeStruct((B, n, GQA_WIDTH), F32),
        scratch_shapes=[pltpu.VMEM((R, LANES), BF16), pltpu.VMEM((LANES + ONES_ROWS, R), F32),
                        pltpu.VMEM((tk, R), F32), pltpu.VMEM((tk, R), F32)],
        compiler_params=_cparams(("parallel", "arbitrary")),
        name="gqa_attn",
    )(qa, ka, vat, km, vmt, bias)


def _mla_kernel(q_ref, k_ref, vt_ref, km_ref, vmt_ref, bias_ref, o_ref, acc_ref,
                s0_ref, s1_ref, *, tq, tk, n_kv):
    outs = []
    for h in range(2):
        sl = slice(h * LANES, (h + 1) * LANES)
        vs = slice(h * MLA_VROWS, (h + 1) * MLA_VROWS)
        outs.append(_flash(lambda: q_ref[0, :, sl],
                           lambda off: k_ref[0, pl.ds(off, tk), sl],
                           lambda off: vt_ref[0, vs, pl.ds(off, tk)],
                           km_ref[:, sl], vmt_ref[vs, :], bias_ref[...], n_kv, tk,
                           MLA_V_DIM, acc_ref, (s0_ref, s1_ref)))
    o_ref[0] = jnp.concatenate(outs, axis=0).T


def _mla_attention(qb, kb, vbt, km, vmt, bias, tq, tk):
    B, n, _ = qb.shape
    pairs = MLA_HEADS // 2
    kern = functools.partial(_mla_kernel, tq=tq, tk=tk, n_kv=n // tk)
    return pl.pallas_call(
        kern,
        grid=(B, pairs, n // tq),
        in_specs=[pl.BlockSpec((1, tq, 2 * LANES), lambda b, j, i: (b, i, j)),
                  pl.BlockSpec((1, n, 2 * LANES), lambda b, j, i: (b, 0, j)),
                  pl.BlockSpec((1, 2 * MLA_VROWS, n), lambda b, j, i: (b, j, 0)),
                  pl.BlockSpec((META_PAD, 2 * LANES), lambda b, j, i: (0, j)),
                  pl.BlockSpec((2 * MLA_VROWS, META_PAD), lambda b, j, i: (j, 0)),
                  _full(bias.shape)],
        out_specs=pl.BlockSpec((1, tq, LANES), lambda b, j, i: (b, i, j)),
        out_shape=jax.ShapeDtypeStruct((B, n, MLA_WIDTH), F32),
        scratch_shapes=[pltpu.VMEM((MLA_VROWS, tq), F32), pltpu.VMEM((tk, tq), F32),
                        pltpu.VMEM((tk, tq), F32)],
        compiler_params=_cparams(("parallel", "parallel", "arbitrary")),
        name="mla_attn",
    )(qb, kb, vbt, km, vmt, bias)


def _post_kernel(oa_ref, ob_ref, x_ref, ga_ref, gb_ref, wout_ref, gffn_ref,
                 wrh_ref, wrl_ref, br_ref, h_ref, u_ref, route_ref):
    o = jnp.concatenate([_rms(oa_ref[...], ga_ref[...]), _rms(ob_ref[...], gb_ref[...])],
                        axis=1).astype(BF16)
    h = x_ref[...] + jnp.dot(o, wout_ref[...], preferred_element_type=F32)
    h_ref[...] = h
    u = _rms(h, gffn_ref[...])
    u_ref[...] = u

    u_hi = u.astype(BF16)
    u_lo = (u - u_hi.astype(F32)).astype(BF16)
    wrh = wrh_ref[...]
    logits = (jnp.dot(u_hi, wrh, preferred_element_type=F32)
              + jnp.dot(u_lo, wrh, preferred_element_type=F32)
              + jnp.dot(u_hi, wrl_ref[...], preferred_element_type=F32)) + br_ref[...]

    lane = lax.broadcasted_iota(jnp.int32, logits.shape, 1)
    far = jnp.int32(LANES)
    neg = jnp.float32(-jnp.inf)
    lg = jnp.where(lane < N_GROUPS, logits, neg)
    mg = jnp.max(lg, axis=1, keepdims=True)
    p_grp = 1.0 / jnp.sum(jnp.exp(lg - mg), axis=1, keepdims=True)
    grp = jnp.min(jnp.where(lg == mg, lane, far), axis=1, keepdims=True)
    e_lo = N_GROUPS + grp * EXPERTS_PER_GROUP
    le = jnp.where((lane >= e_lo) & (lane < e_lo + EXPERTS_PER_GROUP), logits, neg)
    m1 = jnp.max(le, axis=1, keepdims=True)
    i1 = jnp.min(jnp.where(le == m1, lane, far), axis=1, keepdims=True)
    le2 = jnp.where(lane == i1, neg, le)
    m2 = jnp.max(le2, axis=1, keepdims=True)
    i2 = jnp.min(jnp.where(le2 == m2, lane, far), axis=1, keepdims=True)
    t = jnp.exp(m2 - m1)
    g1 = p_grp / (1.0 + t)
    g2 = g1 * t
    e1 = (i1 - N_GROUPS).astype(F32)
    e2 = (i2 - N_GROUPS).astype(F32)
    route_ref[...] = jnp.where(lane == 0, e1, jnp.where(lane == 1, e2, jnp.where(
        lane == 2, g1, jnp.where(lane == 3, g2, 0.0))))


def _post(oa, ob, x2d, pw, tm):
    T = x2d.shape[0]
    row = lambda w: pl.BlockSpec((tm, w), lambda i: (i, 0))
    weights = (pw["ga"], pw["gb"], pw["wout"], pw["gffn"], pw["wrh"], pw["wrl"], pw["br"])
    return pl.pallas_call(
        _post_kernel,
        grid=(T // tm,),
        in_specs=[row(GQA_WIDTH), row(MLA_WIDTH), row(D_MODEL)]
        + [_full(w.shape) for w in weights],
        out_specs=[row(D_MODEL), row(D_MODEL), row(LANES)],
        out_shape=[jax.ShapeDtypeStruct((T, D_MODEL), F32),
                   jax.ShapeDtypeStruct((T, D_MODEL), F32),
                   jax.ShapeDtypeStruct((T, LANES), F32)],
        compiler_params=_cparams(("parallel",)),
        name="post",
    )(oa, ob, x2d, *weights)


def _moe_kernel(be_ref, nused_ref, tok_ref, tokn_ref, dstp_ref, dst_ref, x_hbm, wgu_ref, wd_ref,
                y_hbm, xbuf, ybuf, gsem, ssem, *, blk):
    i = pl.program_id(0)
    nb = pl.num_programs(0)
    slot = i % 2
    n_used = nused_ref[0]

    def gather_copy(t, r, s):
        return pltpu.make_async_copy(x_hbm.at[pl.ds(t, 1), :], xbuf.at[s, pl.ds(r, 1), :],
                                     gsem.at[s])

    def scatter_copy(d, r, s):
        return pltpu.make_async_copy(ybuf.at[s, pl.ds(r, 1), :], y_hbm.at[pl.ds(d, 1), :],
                                     ssem.at[s])

    def start_rows(make, idx_ref, s, unrolled):
        if unrolled:
            for r in range(blk):
                make(idx_ref[0, 0, r], r, s).start()
        else:
            def body(r, c):
                make(idx_ref[0, 0, r], r, s).start()
                return c
            lax.fori_loop(0, blk, body, 0)

    def wait_rows(make, s):
        def body(r, c):
            make(0, r, s).wait()
            return c
        lax.fori_loop(0, blk, body, 0, unroll=8)

    def experts(s):
        xe = xbuf[s].astype(BF16)
        gu = jnp.dot(xe, wgu_ref[0], preferred_element_type=F32)
        g, up = gu[:, :D_EXPERT], gu[:, D_EXPERT:]
        hid = (g * jax.nn.sigmoid(g) * up).astype(BF16)
        ybuf[s] = jnp.dot(hid, wd_ref[0], preferred_element_type=F32)

    @pl.when(i == 0)
    def _():
        start_rows(gather_copy, tok_ref, 0, False)

    wait_rows(gather_copy, slot)

    @pl.when(i >= 2)
    def _():
        wait_rows(scatter_copy, slot)

    @pl.when(i == 0)
    def _():
        start_rows(gather_copy, tokn_ref, 1 - slot, False)
        experts(slot)

    for par in range(2):
        @pl.when((i >= 1) & (i < n_used) & (slot == par))
        def _():
            start_rows(gather_copy, tokn_ref, 1 - par, True)
            experts(par)
            start_rows(scatter_copy, dstp_ref, 1 - par, True)

    @pl.when((i >= 1) & (i >= n_used))
    def _():
        start_rows(gather_copy, tokn_ref, 1 - slot, False)
        ybuf[slot] = jnp.zeros((blk, D_MODEL), F32)
        start_rows(scatter_copy, dstp_ref, 1 - slot, False)

    @pl.when(i == nb - 1)
    def _():
        start_rows(scatter_copy, dst_ref, slot, False)
        wait_rows(gather_copy, 1 - slot)

        @pl.when(i >= 1)
        def _():
            wait_rows(scatter_copy, 1 - slot)
        wait_rows(scatter_copy, slot)


def _moe_plan(eid, blk):
    T = eid.shape[0]
    A = 2 * T
    n_blocks = (A + N_EXPERTS * (blk - 1) + blk - 1) // blk
    P = n_blocks * blk
    e_flat = eid.T.reshape(A)
    onehot = (e_flat[:, None] == jnp.arange(N_EXPERTS, dtype=jnp.int32)[None, :]).astype(jnp.int32)
    csum = jnp.cumsum(onehot, axis=0)
    rank = jnp.sum(csum * onehot, axis=1) - 1
    counts = csum[-1]
    padded = (counts + blk - 1) // blk * blk
    pends = jnp.cumsum(padded)
    pstarts = pends - padded
    slot = jnp.sum(onehot * pstarts[None, :], axis=1) + rank
    a_idx = jnp.arange(A, dtype=jnp.int32)
    buf_a = jnp.full((P,), -1, jnp.int32).at[slot].set(a_idx)
    invalid = buf_a < 0
    buf_tok = jnp.where(invalid, 0, buf_a % T)
    buf_dst = jnp.where(invalid, A + jnp.cumsum(invalid.astype(jnp.int32)) - 1, buf_a)
    block_expert = jnp.minimum(
        jnp.searchsorted(pends, jnp.arange(n_blocks, dtype=jnp.int32) * blk, side="right"),
        N_EXPERTS - 1).astype(jnp.int32)
    n_used = (pends[-1] // blk).astype(jnp.int32).reshape(1)
    return block_expert, n_used, buf_tok, buf_dst, n_blocks


def _moe(u2d, eid, pw, blk):
    block_expert, n_used, buf_tok, buf_dst, n_blocks = _moe_plan(eid, blk)
    P = n_blocks * blk
    tok3 = buf_tok.reshape(n_blocks, 1, blk)
    dst3 = buf_dst.reshape(n_blocks, 1, blk)
    last = n_blocks - 1
    idx_spec = lambda f: pl.BlockSpec((1, 1, blk), f, memory_space=pltpu.SMEM)
    grid_spec = pltpu.PrefetchScalarGridSpec(
        num_scalar_prefetch=2,
        grid=(n_blocks,),
        in_specs=[idx_spec(lambda i, be, nu: (i, 0, 0)),
                  idx_spec(lambda i, be, nu: (jnp.minimum(i + 1, last), 0, 0)),
                  idx_spec(lambda i, be, nu: (jnp.maximum(i - 1, 0), 0, 0)),
                  idx_spec(lambda i, be, nu: (i, 0, 0)),
                  pl.BlockSpec(memory_space=pl.ANY),
                  pl.BlockSpec((1, D_MODEL, 2 * D_EXPERT), lambda i, be, nu: (be[i], 0, 0)),
                  pl.BlockSpec((1, D_EXPERT, D_MODEL), lambda i, be, nu: (be[i], 0, 0))],
        out_specs=pl.BlockSpec(memory_space=pl.ANY),
        scratch_shapes=[pltpu.VMEM((2, blk, D_MODEL), F32), pltpu.VMEM((2, blk, D_MODEL), F32),
                        pltpu.SemaphoreType.DMA((2,)), pltpu.SemaphoreType.DMA((2,))],
    )
    return pl.pallas_call(
        functools.partial(_moe_kernel, blk=blk),
        grid_spec=grid_spec,
        out_shape=jax.ShapeDtypeStruct((P, D_MODEL), F32),
        compiler_params=_cparams(("arbitrary",)),
        name="moe",
    )(block_expert, n_used, tok3, tok3, dst3, dst3, u2d, pw["wgu"], pw["wd"])


def _final_kernel(h_ref, y1_ref, y2_ref, route_ref, g_ref, o_ref):
    r = route_ref[...]
    y = y1_ref[...] * r[:, 2:3] + y2_ref[...] * r[:, 3:4]
    o_ref[...] = _rms(h_ref[...] + y, g_ref[...])


def _final(h2d, y, route, gfin, tm):
    T = h2d.shape[0]
    second = T // tm
    return pl.pallas_call(
        _final_kernel,
        grid=(T // tm,),
        in_specs=[pl.BlockSpec((tm, D_MODEL), lambda i: (i, 0)),
                  pl.BlockSpec((tm, D_MODEL), lambda i: (i, 0)),
                  pl.BlockSpec((tm, D_MODEL), lambda i: (second + i, 0)),
                  pl.BlockSpec((tm, LANES), lambda i: (i, 0)),
                  _full(gfin.shape)],
        out_specs=pl.BlockSpec((tm, D_MODEL), lambda i: (i, 0)),
        out_shape=jax.ShapeDtypeStruct((T, D_MODEL), F32),
        compiler_params=_cparams(("parallel",)),
        name="final",
    )(h2d, y, y, route, gfin)


def _pair_swap(w):
    s = w.shape
    return w.reshape(s[:-1] + (s[-1] // 2, 2))[..., ::-1].reshape(s)


def _prep_weights(norm_mix_g, w_in, gqa_q_norm_g, gqa_k_norm_g, mla_q_norm_g, mla_kv_norm_g,
                  w_mla_q_up, w_mla_kv_up, gqa_out_norm_g, mla_out_norm_g, w_out, norm_ffn_g,
                  w_router_group, b_router_group, w_router_expert, b_router_expert,
                  w_expert_gate, w_expert_up, w_expert_down, norm_final_g):
    D = D_MODEL
    order = jnp.array(GQA_HEAD_ORDER)
    wq = w_in[:, :512].reshape(D, GQA_HEADS, GQA_HEAD_DIM)[:, order].reshape(D, GQA_WIDTH)
    wk = w_in[:, 512:640]
    wkr = w_in[:, 1152:1184]
    z64 = jnp.zeros((D, 64), F32)
    z32 = jnp.zeros((D, 32), F32)
    w1 = jnp.concatenate([
        wq, _pair_swap(wq), wk, _pair_swap(wk), w_in[:, 640:768], w_in[:, 768:1024],
        w_in[:, 1024:1152], jnp.concatenate([z64, wkr, z32], 1),
        jnp.concatenate([z64, _pair_swap(wkr), z32], 1)], axis=1).astype(BF16)

    qu = w_mla_q_up.reshape(MLA_Q_RANK, MLA_HEADS, MLA_QK_DIM)
    zq32 = jnp.zeros((MLA_Q_RANK, MLA_HEADS, 32), F32)
    zq64 = jnp.zeros((MLA_Q_RANK, MLA_HEADS, 64), F32)
    wqu = jnp.concatenate([qu, zq32], axis=2).reshape(MLA_Q_RANK, MLA_HEADS * LANES)
    wqus = jnp.concatenate([zq64, _pair_swap(qu[:, :, MLA_NOPE_DIM:]), zq32],
                           axis=2).reshape(MLA_Q_RANK, MLA_HEADS * LANES)
    kvu = w_mla_kv_up.reshape(MLA_KV_RANK, MLA_HEADS, MLA_NOPE_DIM + MLA_V_DIM)
    wku = jnp.concatenate([kvu[:, :, :MLA_NOPE_DIM],
                           jnp.zeros((MLA_KV_RANK, MLA_HEADS, 64), F32)],
                          axis=2).reshape(MLA_KV_RANK, MLA_HEADS * LANES)
    wvu = kvu[:, :, MLA_NOPE_DIM:].reshape(MLA_KV_RANK, MLA_WIDTH)

    head_block = jnp.full((GQA_HEAD_DIM, GQA_HEAD_DIM), 1.0 / GQA_HEAD_DIM, F32)
    perm_rows = lambda a: a.reshape((GQA_HEADS, GQA_HEAD_DIM) + a.shape[1:])[order].reshape(a.shape)
    wr = jnp.concatenate([w_router_group, w_router_expert,
                          jnp.zeros((D, LANES - N_GROUPS - N_EXPERTS), F32)], axis=1)
    wrh = wr.astype(BF16)
    br = jnp.concatenate([b_router_group, b_router_expert,
                          jnp.zeros((LANES - N_GROUPS - N_EXPERTS,), F32)])[None, :]
    return dict(
        gmix=norm_mix_g[None, :], w1=w1,
        gq=jnp.tile(gqa_q_norm_g, GQA_HEADS)[None, :],
        gqs=jnp.tile(_pair_swap(gqa_q_norm_g), GQA_HEADS)[None, :],
        gk=jnp.tile(gqa_k_norm_g, GQA_KV_HEADS)[None, :],
        gks=jnp.tile(_pair_swap(gqa_k_norm_g), GQA_KV_HEADS)[None, :],
        gcq=mla_q_norm_g[None, :], gckv=mla_kv_norm_g[None, :],
        wqu=wqu.astype(BF16), wqus=wqus.astype(BF16), wku=wku.astype(BF16),
        wvu=wvu.astype(BF16),
        bdq=jnp.kron(jnp.eye(GQA_HEADS, dtype=F32), head_block).astype(BF16),
        bdk=jnp.kron(jnp.eye(GQA_KV_HEADS, dtype=F32), head_block).astype(BF16),
        ga=perm_rows(gqa_out_norm_g)[None, :], gb=mla_out_norm_g[None, :],
        wout=jnp.concatenate([perm_rows(w_out[:GQA_WIDTH]), w_out[GQA_WIDTH:]], 0).astype(BF16),
        gffn=norm_ffn_g[None, :], wrh=wrh, wrl=(wr - wrh.astype(F32)).astype(BF16), br=br,
        wgu=jnp.concatenate([w_expert_gate, w_expert_up], axis=2).astype(BF16),
        wd=w_expert_down.astype(BF16), gfin=norm_final_g[None, :])


def _angles(rows, cols, rot_dim):
    half = rot_dim // 2
    inv = ROPE_THETA ** (-jnp.arange(0, half, 2, dtype=F32) / half)
    ang = jnp.concatenate([rows.astype(F32)[:, None] * inv,
                           cols.astype(F32)[:, None] * inv], axis=-1)
    return jnp.cos(ang), jnp.sin(ang)


def _rope_tables(rows, cols):
    L = rows.shape[0]
    lanes = lambda a: jnp.repeat(a, 2, axis=-1)
    sign = lambda w: jnp.tile(jnp.array([-1.0, 1.0], F32), w // 2)[None, :]
    cg, sg = _angles(rows, cols, GQA_HEAD_DIM)
    cg = jnp.tile(lanes(cg), (1, 2))
    sg = jnp.tile(lanes(sg) * sign(GQA_HEAD_DIM), (1, 2))
    cm, sm = _angles(rows, cols, MLA_ROPE_DIM)
    cm, sm = lanes(cm), lanes(sm) * sign(MLA_ROPE_DIM)
    one64, z64, z32 = jnp.ones((L, 64), F32), jnp.zeros((L, 64), F32), jnp.zeros((L, 32), F32)
    cmq = jnp.concatenate([one64, cm, z32], 1) * SCALE_B
    smq = jnp.concatenate([z64, sm, z32], 1) * SCALE_B
    cmk = jnp.concatenate([z64, cm, z32], 1)
    smk = jnp.concatenate([z64, sm, z32], 1)
    return (cg, sg, cmq, smq, cmk, smk)


def _tile_sizes(n):
    tm = min(256, n)
    tq_gqa = min(128, n)
    tq_mla = min(1024, n)
    tk = min(512, n // KV_UNROLL)
    return tm, tq_gqa, tq_mla, tk


MOE_ROWS = 256


def _with_ones(vt, groups):
    lead, (rows, n) = vt.shape[:-2], vt.shape[-2:]
    g = vt.reshape(lead + (groups, rows // groups, n))
    ones = jnp.ones(lead + (groups, ONES_ROWS, n), vt.dtype)
    return jnp.concatenate([g, ones], axis=-2).reshape(lead + (rows + groups * ONES_ROWS, n))


def _trunk(x, pw, meta_kv, bias):
    B, n, D = x.shape
    T = B * n
    tm, tq_gqa, tq_mla, tk = _tile_sizes(n)
    t = jnp.arange(n, dtype=jnp.int32)
    tables = _rope_tables(t // GRID_W, t % GRID_W)
    x2d = x.reshape(T, D)
    qa, ka, va, qb, kb, vb = _project(x2d, tables, pw, tm)
    ka_m, vat_m, kb_m, vbt_m = meta_kv
    r3 = lambda a: a.reshape(B, n, a.shape[-1])
    t3 = lambda a: jnp.swapaxes(r3(a), 1, 2)
    oa = _gqa_attention(r3(qa), r3(ka), _with_ones(t3(va), 1), ka_m, vat_m, bias, tq_gqa, tk)
    ob = _mla_attention(r3(qb), r3(kb), _with_ones(t3(vb), MLA_HEADS), kb_m, vbt_m, bias,
                        tq_mla, tk)
    h, u, route = _post(oa.reshape(T, GQA_WIDTH), ob.reshape(T, MLA_WIDTH), x2d, pw, tm)
    eid = route[:, :2].astype(jnp.int32)
    y = _moe(u, eid, pw, MOE_ROWS)
    out = _final(h, y, route, pw["gfin"], tm)
    return out.reshape(B, n, D)


def kernel(x_prompt, x_sample, meta_tokens, norm_mix_g, w_in, gqa_q_norm_g, gqa_k_norm_g, mla_q_norm_g, mla_kv_norm_g, w_mla_q_up, w_mla_kv_up, gqa_out_norm_g, mla_out_norm_g, w_out, norm_ffn_g, w_router_group, b_router_group, w_router_expert, b_router_expert, w_expert_gate, w_expert_up, w_expert_down, norm_final_g):
    pw = _prep_weights(norm_mix_g[0], w_in[0], gqa_q_norm_g[0], gqa_k_norm_g[0],
                       mla_q_norm_g[0], mla_kv_norm_g[0], w_mla_q_up[0], w_mla_kv_up[0],
                       gqa_out_norm_g[0], mla_out_norm_g[0], w_out[0], norm_ffn_g[0],
                       w_router_group[0], b_router_group[0], w_router_expert[0],
                       b_router_expert[0], w_expert_gate[0], w_expert_up[0],
                       w_expert_down[0], norm_final_g)
    meta_x = jnp.concatenate([meta_tokens.astype(F32),
                              jnp.zeros((META_PAD - N_META, D_MODEL), F32)], axis=0)
    mpos = jnp.arange(META_PAD, dtype=jnp.int32)
    meta_tables = _rope_tables(jnp.full((META_PAD,), -1, jnp.int32),
                               jnp.where(mpos < N_META, mpos, 0))
    _, ka_m, va_m, _, kb_m, vb_m = _project(meta_x, meta_tables, pw, META_PAD)
    bias = jnp.where(mpos < N_META, 0.0, NEG_BIG).astype(F32)[:, None]
    meta_kv = (ka_m, _with_ones(va_m.T, 1), kb_m, _with_ones(vb_m.T, MLA_HEADS))
    return (_trunk(x_prompt, pw, meta_kv, bias), _trunk(x_sample, pw, meta_kv, bias))
```

```python
import functools
import math

import jax
import jax.numpy as jnp
from jax import lax
from jax.experimental import pallas as pl
from jax.experimental.pallas import tpu as pltpu

F32 = jnp.float32
BF16 = jnp.bfloat16

D_MODEL = 1024
N_META = 16
GRID_W = 64
ROPE_THETA = 10000.0
NORM_EPS = 1e-6

GQA_HEADS = 8
GQA_KV_HEADS = 2
GQA_HEAD_DIM = 64
GQA_WIDTH = GQA_HEADS * GQA_HEAD_DIM

MLA_HEADS = 8
MLA_Q_RANK = 256
MLA_KV_RANK = 128
MLA_NOPE_DIM = 64
MLA_ROPE_DIM = 32
MLA_V_DIM = 64
MLA_QK_DIM = MLA_NOPE_DIM + MLA_ROPE_DIM
MLA_WIDTH = MLA_HEADS * MLA_V_DIM

N_GROUPS = 4
EXPERTS_PER_GROUP = 8
N_EXPERTS = N_GROUPS * EXPERTS_PER_GROUP
D_EXPERT = 256

LANES = 128
HALF = LANES // 2
META_PAD = 128
ONES_ROWS = 16
MLA_VROWS = MLA_V_DIM + ONES_ROWS
KV_UNROLL = 8
NEG_BIG = -1e30
VMEM_LIMIT = 56 * 1024 * 1024

LOG2E = 1.4426950408889634
SCALE_A = GQA_HEAD_DIM ** -0.5 * LOG2E
SCALE_B = MLA_QK_DIM ** -0.5 * LOG2E

GQA_HEAD_ORDER = (0, 4, 1, 5, 2, 6, 3, 7)

_C_Q, _C_QS, _C_K, _C_KS, _C_V, _C_CQ, _C_CKV, _C_KR, _C_KRS, _C_END = (
    0, 512, 1024, 1152, 1280, 1408, 1664, 1792, 1920, 2048)

NT_DIMS = (((1,), (1,)), ((), ()))


def _cparams(sem):
    return pltpu.CompilerParams(dimension_semantics=sem, vmem_limit_bytes=VMEM_LIMIT)


def _full(shape):
    nd = len(shape)
    return pl.BlockSpec(shape, lambda *_: (0,) * nd)


def _rms(x, g):
    return x * lax.rsqrt(jnp.mean(x * x, axis=-1, keepdims=True) + NORM_EPS) * g


def _tile_lanes(a, reps):
    return jnp.concatenate([a] * reps, axis=1)


ROW_TILE = (D_MODEL // LANES, LANES)


def _row_tiles_spec(rows, index_map):
    return pl.BlockSpec((rows,) + ROW_TILE, index_map)


def _store_row_tiles(ref, x):
    for j in range(ROW_TILE[0]):
        ref[:, j, :] = x[:, j * LANES:(j + 1) * LANES]


def _load_row_tiles(ref):
    return jnp.concatenate([ref[:, j, :] for j in range(ROW_TILE[0])], axis=1)


def _head_mean_sq(z, bd):
    sq = z * z
    hi = sq.astype(BF16)
    lo = (sq - hi.astype(F32)).astype(BF16)
    return (jnp.dot(hi, bd, preferred_element_type=F32)
            + jnp.dot(lo, bd, preferred_element_type=F32))


def _proj_kernel(x_ref, cg_ref, sg_ref, cmq_ref, smq_ref, cmk_ref, smk_ref,
                 gmix_ref, w1_ref, gq_ref, gqs_ref, gk_ref, gks_ref, gcq_ref, gckv_ref,
                 wqu_ref, wqus_ref, wku_ref, wvu_ref, bdq_ref, bdk_ref,
                 qa_ref, ka_ref, va_ref, qb_ref, kb_ref, vb_ref):
    u = _rms(x_ref[...], gmix_ref[...]).astype(BF16)
    z = jnp.dot(u, w1_ref[...], preferred_element_type=F32)

    cg, sg = cg_ref[...], sg_ref[...]
    zq, zqs = z[:, _C_Q:_C_QS], z[:, _C_QS:_C_K]
    rq = lax.rsqrt(_head_mean_sq(zq, bdq_ref[...]) + NORM_EPS) * SCALE_A
    c4, s4 = _tile_lanes(cg, 4), _tile_lanes(sg, 4)
    qa_ref[...] = (rq * (zq * gq_ref[...] * c4 + zqs * gqs_ref[...] * s4)).astype(BF16)
    zk, zks = z[:, _C_K:_C_KS], z[:, _C_KS:_C_V]
    rk = lax.rsqrt(_head_mean_sq(zk, bdk_ref[...]) + NORM_EPS)
    ka_ref[...] = (rk * (zk * gk_ref[...] * cg + zks * gks_ref[...] * sg)).astype(BF16)
    va_ref[...] = z[:, _C_V:_C_CQ].astype(BF16)

    cqn = _rms(z[:, _C_CQ:_C_CKV], gcq_ref[...]).astype(BF16)
    zqb = jnp.dot(cqn, wqu_ref[...], preferred_element_type=F32)
    zqbs = jnp.dot(cqn, wqus_ref[...], preferred_element_type=F32)
    qb_ref[...] = (zqb * _tile_lanes(cmq_ref[...], MLA_HEADS)
                   + zqbs * _tile_lanes(smq_ref[...], MLA_HEADS)).astype(BF16)
    ckvn = _rms(z[:, _C_CKV:_C_KR], gckv_ref[...]).astype(BF16)
    kn = jnp.dot(ckvn, wku_ref[...], preferred_element_type=F32)
    kr = z[:, _C_KR:_C_KRS] * cmk_ref[...] + z[:, _C_KRS:_C_END] * smk_ref[...]
    kb_ref[...] = (kn + _tile_lanes(kr, MLA_HEADS)).astype(BF16)
    vb_ref[...] = jnp.dot(ckvn, wvu_ref[...], preferred_element_type=F32).astype(BF16)


def _project(x2d, tables, pw, tm):
    T = x2d.shape[0]
    n_pos = tables[0].shape[0]
    pos_blocks = n_pos // tm
    row = lambda w: pl.BlockSpec((tm, w), lambda i: (i, 0))
    tab = pl.BlockSpec((tm, LANES), lambda i: (i % pos_blocks, 0))
    weights = (pw["gmix"], pw["w1"], pw["gq"], pw["gqs"], pw["gk"], pw["gks"], pw["gcq"],
               pw["gckv"], pw["wqu"], pw["wqus"], pw["wku"], pw["wvu"], pw["bdq"], pw["bdk"])
    widths = (GQA_WIDTH, LANES, LANES, MLA_HEADS * LANES, MLA_HEADS * LANES, MLA_WIDTH)
    return pl.pallas_call(
        _proj_kernel,
        grid=(T // tm,),
        in_specs=[row(D_MODEL)] + [tab] * 6 + [_full(w.shape) for w in weights],
        out_specs=[row(w) for w in widths],
        out_shape=[jax.ShapeDtypeStruct((T, w), BF16) for w in widths],
        compiler_params=_cparams(("parallel",)),
        name="proj",
    )(x2d, *tables, *weights)


def _flash(load_q, load_k, load_vt, km, vmt, bias, n_kv, tk, vdim, acc_ref, s_refs):
    assert n_kv >= KV_UNROLL and n_kv % KV_UNROLL == 0 and KV_UNROLL % 2 == 0

    def scores(j):
        off = pl.multiple_of(j * tk, tk)
        return lax.dot_general(load_k(off), load_q(), NT_DIMS, preferred_element_type=F32)

    def consume(s, j, m_prev):
        off = pl.multiple_of(j * tk, tk)
        m_new = jnp.maximum(m_prev, jnp.max(s, axis=0, keepdims=True))
        alpha = jnp.exp2(m_prev - m_new)
        p = jnp.exp2(s - m_new)
        acc_ref[...] = alpha * acc_ref[...] + jnp.dot(
            load_vt(off), p.astype(BF16), preferred_element_type=F32)
        return m_new

    s = lax.dot_general(km, load_q(), NT_DIMS, preferred_element_type=F32) + bias
    m0 = jnp.max(s, axis=0, keepdims=True)
    acc_ref[...] = jnp.dot(vmt, jnp.exp2(s - m0).astype(BF16), preferred_element_type=F32)
    s_refs[0][...] = scores(0)

    def group(i, m, last):
        j0 = KV_UNROLL * i
        for u in range(KV_UNROLL):
            if not (last and u == KV_UNROLL - 1):
                s_refs[(u + 1) % 2][...] = scores(j0 + u + 1)
            m = consume(s_refs[u % 2][...], j0 + u, m)
        return m

    trips = n_kv // KV_UNROLL
    m = lax.fori_loop(0, trips - 1, lambda i, c: group(i, c, False), m0)
    group(trips - 1, m, True)
    acc = acc_ref[...]
    return acc[:vdim] * (1.0 / acc[vdim:vdim + 1])


def _gqa_kernel(q_ref, k_ref, vt_ref, km_ref, vmt_ref, bias_ref, o_ref,
                qs_ref, acc_ref, s0_ref, s1_ref, *, tq, tk, n_kv):
    lo = lax.broadcasted_iota(jnp.int32, (tq, LANES), 1) < HALF
    zero = jnp.zeros((tq, LANES), BF16)
    for c in range(GQA_HEADS // 2):
        qc = q_ref[0, :, c * LANES:(c + 1) * LANES]
        qs_ref[(2 * c) * tq:(2 * c + 1) * tq, :] = jnp.where(lo, qc, zero)
        qs_ref[(2 * c + 1) * tq:(2 * c + 2) * tq, :] = jnp.where(lo, zero, qc)
    ot = _flash(lambda: qs_ref[...],
                lambda off: k_ref[0, pl.ds(off, tk), :],
                lambda off: vt_ref[0, :, pl.ds(off, tk)],
                km_ref[...], vmt_ref[...], bias_ref[...], n_kv, tk, LANES,
                acc_ref, (s0_ref, s1_ref))
    for c in range(GQA_HEADS // 2):
        pair = jnp.concatenate([ot[:HALF, (2 * c) * tq:(2 * c + 1) * tq],
                                ot[HALF:, (2 * c + 1) * tq:(2 * c + 2) * tq]], axis=0)
        o_ref[0, :, c * LANES:(c + 1) * LANES] = pair.T


def _gqa_attention(qa, ka, vat, km, vmt, bias, tq, tk):
    B, n, _ = qa.shape
    R = GQA_HEADS * tq
    kern = functools.partial(_gqa_kernel, tq=tq, tk=tk, n_kv=n // tk)
    return pl.pallas_call(
        kern,
        grid=(B, n // tq),
        in_specs=[pl.BlockSpec((1, tq, GQA_WIDTH), lambda b, i: (b, i, 0)),
                  pl.BlockSpec((1, n, LANES), lambda b, i: (b, 0, 0)),
                  pl.BlockSpec((1, LANES + ONES_ROWS, n), lambda b, i: (b, 0, 0)),
                  _full(km.shape), _full(vmt.shape), _full(bias.shape)],
        out_specs=pl.BlockSpec((1, tq, GQA_WIDTH), lambda b, i: (b, i, 0)),
        out_shape=jax.ShapeDtypeStruct((B, n, GQA_WIDTH), F32),
        scratch_shapes=[pltpu.VMEM((R, LANES), BF16), pltpu.VMEM((LANES + ONES_ROWS, R), F32),
                        pltpu.VMEM((tk, R), F32), pltpu.VMEM((tk, R), F32)],
        compiler_params=_cparams(("parallel", "arbitrary")),
        name="gqa_attn",
    )(qa, ka, vat, km, vmt, bias)


def _mla_kernel(q_ref, k_ref, vt_ref, km_ref, vmt_ref, bias_ref, o_ref, acc_ref,
                s0_ref, s1_ref, *, tq, tk, n_kv):
    outs = []
    for h in range(2):
        sl = slice(h * LANES, (h + 1) * LANES)
        vs = slice(h * MLA_VROWS, (h + 1) * MLA_VROWS)
        outs.append(_flash(lambda: q_ref[0, :, sl],
                           lambda off: k_ref[0, pl.ds(off, tk), sl],
                           lambda off: vt_ref[0, vs, pl.ds(off, tk)],
                           km_ref[:, sl], vmt_ref[vs, :], bias_ref[...], n_kv, tk,
                           MLA_V_DIM, acc_ref, (s0_ref, s1_ref)))
    o_ref[0] = jnp.concatenate(outs, axis=0).T


def _mla_attention(qb, kb, vbt, km, vmt, bias, tq, tk):
    B, n, _ = qb.shape
    pairs = MLA_HEADS // 2
    kern = functools.partial(_mla_kernel, tq=tq, tk=tk, n_kv=n // tk)
    return pl.pallas_call(
        kern,
        grid=(B, pairs, n // tq),
        in_specs=[pl.BlockSpec((1, tq, 2 * LANES), lambda b, j, i: (b, i, j)),
                  pl.BlockSpec((1, n, 2 * LANES), lambda b, j, i: (b, 0, j)),
                  pl.BlockSpec((1, 2 * MLA_VROWS, n), lambda b, j, i: (b, j, 0)),
                  pl.BlockSpec((META_PAD, 2 * LANES), lambda b, j, i: (0, j)),
                  pl.BlockSpec((2 * MLA_VROWS, META_PAD), lambda b, j, i: (j, 0)),
                  _full(bias.shape)],
        out_specs=pl.BlockSpec((1, tq, LANES), lambda b, j, i: (b, i, j)),
        out_shape=jax.ShapeDtypeStruct((B, n, MLA_WIDTH), F32),
        scratch_shapes=[pltpu.VMEM((MLA_VROWS, tq), F32), pltpu.VMEM((tk, tq), F32),
                        pltpu.VMEM((tk, tq), F32)],
        compiler_params=_cparams(("parallel", "parallel", "arbitrary")),
        name="mla_attn",
    )(qb, kb, vbt, km, vmt, bias)


def _post_kernel(oa_ref, ob_ref, x_ref, ga_ref, gb_ref, wout_ref, gffn_ref,
                 wrh_ref, wrl_ref, br_ref, h_ref, u_ref, route_ref):
    o = jnp.concatenate([_rms(oa_ref[...], ga_ref[...]), _rms(ob_ref[...], gb_ref[...])],
                        axis=1).astype(BF16)
    h = x_ref[...] + jnp.dot(o, wout_ref[...], preferred_element_type=F32)
    h_ref[...] = h
    u = _rms(h, gffn_ref[...])
    _store_row_tiles(u_ref, u)

    u_hi = u.astype(BF16)
    u_lo = (u - u_hi.astype(F32)).astype(BF16)
    wrh = wrh_ref[...]
    logits = (jnp.dot(u_hi, wrh, preferred_element_type=F32)
              + jnp.dot(u_lo, wrh, preferred_element_type=F32)
              + jnp.dot(u_hi, wrl_ref[...], preferred_element_type=F32)) + br_ref[...]

    lane = lax.broadcasted_iota(jnp.int32, logits.shape, 1)
    far = jnp.int32(LANES)
    neg = jnp.float32(-jnp.inf)
    lg = jnp.where(lane < N_GROUPS, logits, neg)
    mg = jnp.max(lg, axis=1, keepdims=True)
    p_grp = 1.0 / jnp.sum(jnp.exp(lg - mg), axis=1, keepdims=True)
    grp = jnp.min(jnp.where(lg == mg, lane, far), axis=1, keepdims=True)
    e_lo = N_GROUPS + grp * EXPERTS_PER_GROUP
    le = jnp.where((lane >= e_lo) & (lane < e_lo + EXPERTS_PER_GROUP), logits, neg)
    m1 = jnp.max(le, axis=1, keepdims=True)
    i1 = jnp.min(jnp.where(le == m1, lane, far), axis=1, keepdims=True)
    le2 = jnp.where(lane == i1, neg, le)
    m2 = jnp.max(le2, axis=1, keepdims=True)
    i2 = jnp.min(jnp.where(le2 == m2, lane, far), axis=1, keepdims=True)
    t = jnp.exp(m2 - m1)
    g1 = p_grp / (1.0 + t)
    g2 = g1 * t
    e1 = (i1 - N_GROUPS).astype(F32)
    e2 = (i2 - N_GROUPS).astype(F32)
    route_ref[...] = jnp.where(lane == 0, e1, jnp.where(lane == 1, e2, jnp.where(
        lane == 2, g1, jnp.where(lane == 3, g2, 0.0))))


def _post(oa, ob, x2d, pw, tm):
    T = x2d.shape[0]
    row = lambda w: pl.BlockSpec((tm, w), lambda i: (i, 0))
    weights = (pw["ga"], pw["gb"], pw["wout"], pw["gffn"], pw["wrh"], pw["wrl"], pw["br"])
    return pl.pallas_call(
        _post_kernel,
        grid=(T // tm,),
        in_specs=[row(GQA_WIDTH), row(MLA_WIDTH), row(D_MODEL)]
        + [_full(w.shape) for w in weights],
        out_specs=[row(D_MODEL), _row_tiles_spec(tm, lambda i: (i, 0, 0)), row(LANES)],
        out_shape=[jax.ShapeDtypeStruct((T, D_MODEL), F32),
                   jax.ShapeDtypeStruct((T,) + ROW_TILE, F32),
                   jax.ShapeDtypeStruct((T, LANES), F32)],
        compiler_params=_cparams(("parallel",)),
        name="post",
    )(oa, ob, x2d, *weights)


def _moe_kernel(be_ref, nused_ref, tok_ref, tokn_ref, dstp_ref, dst_ref, x_hbm, wgu_ref, wd_ref,
                y_hbm, xbuf, ybuf, gsem, ssem, *, blk):
    i = pl.program_id(0)
    nb = pl.num_programs(0)
    slot = i % 2
    n_used = nused_ref[0]

    def gather_copy(t, r, s):
        return pltpu.make_async_copy(x_hbm.at[pl.ds(t, 1)], xbuf.at[s, pl.ds(r, 1)], gsem.at[s])

    def scatter_copy(d, r, s):
        return pltpu.make_async_copy(ybuf.at[s, pl.ds(r, 1)], y_hbm.at[pl.ds(d, 1)], ssem.at[s])

    def start_rows(make, idx_ref, s, unrolled):
        if unrolled:
            for r in range(blk):
                make(idx_ref[0, 0, r], r, s).start()
        else:
            def body(r, c):
                make(idx_ref[0, 0, r], r, s).start()
                return c
            lax.fori_loop(0, blk, body, 0)

    def wait_rows(make, s):
        def body(r, c):
            make(0, r, s).wait()
            return c
        lax.fori_loop(0, blk, body, 0, unroll=8)

    def experts(s):
        xe = _load_row_tiles(xbuf.at[s]).astype(BF16)
        gu = jnp.dot(xe, wgu_ref[0], preferred_element_type=F32)
        g, up = gu[:, :D_EXPERT], gu[:, D_EXPERT:]
        hid = (g * jax.nn.sigmoid(g) * up).astype(BF16)
        _store_row_tiles(ybuf.at[s], jnp.dot(hid, wd_ref[0], preferred_element_type=F32))

    @pl.when(i == 0)
    def _():
        start_rows(gather_copy, tok_ref, 0, False)

    wait_rows(gather_copy, slot)

    @pl.when(i >= 2)
    def _():
        wait_rows(scatter_copy, slot)

    @pl.when(i == 0)
    def _():
        start_rows(gather_copy, tokn_ref, 1 - slot, False)
        experts(slot)

    for par in range(2):
        @pl.when((i >= 1) & (i < n_used) & (slot == par))
        def _():
            start_rows(gather_copy, tokn_ref, 1 - par, True)
            experts(par)
            start_rows(scatter_copy, dstp_ref, 1 - par, True)

    @pl.when((i >= 1) & (i >= n_used))
    def _():
        start_rows(gather_copy, tokn_ref, 1 - slot, False)
        ybuf[slot] = jnp.zeros((blk,) + ROW_TILE, F32)
        start_rows(scatter_copy, dstp_ref, 1 - slot, False)

    @pl.when(i == nb - 1)
    def _():
        start_rows(scatter_copy, dst_ref, slot, False)
        wait_rows(gather_copy, 1 - slot)

        @pl.when(i >= 1)
        def _():
            wait_rows(scatter_copy, 1 - slot)
        wait_rows(scatter_copy, slot)


def _moe_plan(eid, blk):
    T = eid.shape[0]
    A = 2 * T
    n_blocks = (A + N_EXPERTS * (blk - 1) + blk - 1) // blk
    P = n_blocks * blk
    e_flat = eid.T.reshape(A)
    onehot = (e_flat[:, None] == jnp.arange(N_EXPERTS, dtype=jnp.int32)[None, :]).astype(jnp.int32)
    csum = jnp.cumsum(onehot, axis=0)
    rank = jnp.sum(csum * onehot, axis=1) - 1
    counts = csum[-1]
    padded = (counts + blk - 1) // blk * blk
    pends = jnp.cumsum(padded)
    pstarts = pends - padded
    slot = jnp.sum(onehot * pstarts[None, :], axis=1) + rank
    a_idx = jnp.arange(A, dtype=jnp.int32)
    buf_a = jnp.full((P,), -1, jnp.int32).at[slot].set(a_idx)
    invalid = buf_a < 0
    buf_tok = jnp.where(invalid, 0, buf_a % T)
    buf_dst = jnp.where(invalid, A + jnp.cumsum(invalid.astype(jnp.int32)) - 1, buf_a)
    block_expert = jnp.minimum(
        jnp.searchsorted(pends, jnp.arange(n_blocks, dtype=jnp.int32) * blk, side="right"),
        N_EXPERTS - 1).astype(jnp.int32)
    n_used = (pends[-1] // blk).astype(jnp.int32).reshape(1)
    return block_expert, n_used, buf_tok, buf_dst, n_blocks


def _moe(u2d, eid, pw, blk):
    block_expert, n_used, buf_tok, buf_dst, n_blocks = _moe_plan(eid, blk)
    P = n_blocks * blk
    tok3 = buf_tok.reshape(n_blocks, 1, blk)
    dst3 = buf_dst.reshape(n_blocks, 1, blk)
    last = n_blocks - 1
    idx_spec = lambda f: pl.BlockSpec((1, 1, blk), f, memory_space=pltpu.SMEM)
    grid_spec = pltpu.PrefetchScalarGridSpec(
        num_scalar_prefetch=2,
        grid=(n_blocks,),
        in_specs=[idx_spec(lambda i, be, nu: (i, 0, 0)),
                  idx_spec(lambda i, be, nu: (jnp.minimum(i + 1, last), 0, 0)),
                  idx_spec(lambda i, be, nu: (jnp.maximum(i - 1, 0), 0, 0)),
                  idx_spec(lambda i, be, nu: (i, 0, 0)),
                  pl.BlockSpec(memory_space=pl.ANY),
                  pl.BlockSpec((1, D_MODEL, 2 * D_EXPERT), lambda i, be, nu: (be[i], 0, 0)),
                  pl.BlockSpec((1, D_EXPERT, D_MODEL), lambda i, be, nu: (be[i], 0, 0))],
        out_specs=pl.BlockSpec(memory_space=pl.ANY),
        scratch_shapes=[pltpu.VMEM((2, blk) + ROW_TILE, F32), pltpu.VMEM((2, blk) + ROW_TILE, F32),
                        pltpu.SemaphoreType.DMA((2,)), pltpu.SemaphoreType.DMA((2,))],
    )
    return pl.pallas_call(
        functools.partial(_moe_kernel, blk=blk),
        grid_spec=grid_spec,
        out_shape=jax.ShapeDtypeStruct((P,) + ROW_TILE, F32),
        compiler_params=_cparams(("arbitrary",)),
        name="moe",
    )(block_expert, n_used, tok3, tok3, dst3, dst3, u2d, pw["wgu"], pw["wd"])


def _final_kernel(h_ref, y1_ref, y2_ref, route_ref, g_ref, o_ref):
    r = route_ref[...]
    y = _load_row_tiles(y1_ref) * r[:, 2:3] + _load_row_tiles(y2_ref) * r[:, 3:4]
    o_ref[...] = _rms(h_ref[...] + y, g_ref[...])


def _final(h2d, y, route, gfin, tm):
    T = h2d.shape[0]
    second = T // tm
    return pl.pallas_call(
        _final_kernel,
        grid=(T // tm,),
        in_specs=[pl.BlockSpec((tm, D_MODEL), lambda i: (i, 0)),
                  _row_tiles_spec(tm, lambda i: (i, 0, 0)),
                  _row_tiles_spec(tm, lambda i: (second + i, 0, 0)),
                  pl.BlockSpec((tm, LANES), lambda i: (i, 0)),
                  _full(gfin.shape)],
        out_specs=pl.BlockSpec((tm, D_MODEL), lambda i: (i, 0)),
        out_shape=jax.ShapeDtypeStruct((T, D_MODEL), F32),
        compiler_params=_cparams(("parallel",)),
        name="final",
    )(h2d, y, y, route, gfin)


def _pair_swap(w):
    s = w.shape
    return w.reshape(s[:-1] + (s[-1] // 2, 2))[..., ::-1].reshape(s)


def _prep_weights(norm_mix_g, w_in, gqa_q_norm_g, gqa_k_norm_g, mla_q_norm_g, mla_kv_norm_g,
                  w_mla_q_up, w_mla_kv_up, gqa_out_norm_g, mla_out_norm_g, w_out, norm_ffn_g,
                  w_router_group, b_router_group, w_router_expert, b_router_expert,
                  w_expert_gate, w_expert_up, w_expert_down, norm_final_g):
    D = D_MODEL
    order = jnp.array(GQA_HEAD_ORDER)
    wq = w_in[:, :512].reshape(D, GQA_HEADS, GQA_HEAD_DIM)[:, order].reshape(D, GQA_WIDTH)
    wk = w_in[:, 512:640]
    wkr = w_in[:, 1152:1184]
    z64 = jnp.zeros((D, 64), F32)
    z32 = jnp.zeros((D, 32), F32)
    w1 = jnp.concatenate([
        wq, _pair_swap(wq), wk, _pair_swap(wk), w_in[:, 640:768], w_in[:, 768:1024],
        w_in[:, 1024:1152], jnp.concatenate([z64, wkr, z32], 1),
        jnp.concatenate([z64, _pair_swap(wkr), z32], 1)], axis=1).astype(BF16)

    qu = w_mla_q_up.reshape(MLA_Q_RANK, MLA_HEADS, MLA_QK_DIM)
    zq32 = jnp.zeros((MLA_Q_RANK, MLA_HEADS, 32), F32)
    zq64 = jnp.zeros((MLA_Q_RANK, MLA_HEADS, 64), F32)
    wqu = jnp.concatenate([qu, zq32], axis=2).reshape(MLA_Q_RANK, MLA_HEADS * LANES)
    wqus = jnp.concatenate([zq64, _pair_swap(qu[:, :, MLA_NOPE_DIM:]), zq32],
                           axis=2).reshape(MLA_Q_RANK, MLA_HEADS * LANES)
    kvu = w_mla_kv_up.reshape(MLA_KV_RANK, MLA_HEADS, MLA_NOPE_DIM + MLA_V_DIM)
    wku = jnp.concatenate([kvu[:, :, :MLA_NOPE_DIM],
                           jnp.zeros((MLA_KV_RANK, MLA_HEADS, 64), F32)],
                          axis=2).reshape(MLA_KV_RANK, MLA_HEADS * LANES)
    wvu = kvu[:, :, MLA_NOPE_DIM:].reshape(MLA_KV_RANK, MLA_WIDTH)

    head_block = jnp.full((GQA_HEAD_DIM, GQA_HEAD_DIM), 1.0 / GQA_HEAD_DIM, F32)
    perm_rows = lambda a: a.reshape((GQA_HEADS, GQA_HEAD_DIM) + a.shape[1:])[order].reshape(a.shape)
    wr = jnp.concatenate([w_router_group, w_router_expert,
                          jnp.zeros((D, LANES - N_GROUPS - N_EXPERTS), F32)], axis=1)
    wrh = wr.astype(BF16)
    br = jnp.concatenate([b_router_group, b_router_expert,
                          jnp.zeros((LANES - N_GROUPS - N_EXPERTS,), F32)])[None, :]
    return dict(
        gmix=norm_mix_g[None, :], w1=w1,
        gq=jnp.tile(gqa_q_norm_g, GQA_HEADS)[None, :],
        gqs=jnp.tile(_pair_swap(gqa_q_norm_g), GQA_HEADS)[None, :],
        gk=jnp.tile(gqa_k_norm_g, GQA_KV_HEADS)[None, :],
        gks=jnp.tile(_pair_swap(gqa_k_norm_g), GQA_KV_HEADS)[None, :],
        gcq=mla_q_norm_g[None, :], gckv=mla_kv_norm_g[None, :],
        wqu=wqu.astype(BF16), wqus=wqus.astype(BF16), wku=wku.astype(BF16),
        wvu=wvu.astype(BF16),
        bdq=jnp.kron(jnp.eye(GQA_HEADS, dtype=F32), head_block).astype(BF16),
        bdk=jnp.kron(jnp.eye(GQA_KV_HEADS, dtype=F32), head_block).astype(BF16),
        ga=perm_rows(gqa_out_norm_g)[None, :], gb=mla_out_norm_g[None, :],
        wout=jnp.concatenate([perm_rows(w_out[:GQA_WIDTH]), w_out[GQA_WIDTH:]], 0).astype(BF16),
        gffn=norm_ffn_g[None, :], wrh=wrh, wrl=(wr - wrh.astype(F32)).astype(BF16), br=br,
        wgu=jnp.concatenate([w_expert_gate, w_expert_up], axis=2).astype(BF16),
        wd=w_expert_down.astype(BF16), gfin=norm_final_g[None, :])


def _angles(rows, cols, rot_dim):
    half = rot_dim // 2
    inv = ROPE_THETA ** (-jnp.arange(0, half, 2, dtype=F32) / half)
    ang = jnp.concatenate([rows.astype(F32)[:, None] * inv,
                           cols.astype(F32)[:, None] * inv], axis=-1)
    return jnp.cos(ang), jnp.sin(ang)


def _rope_tables(rows, cols):
    L = rows.shape[0]
    lanes = lambda a: jnp.repeat(a, 2, axis=-1)
    sign = lambda w: jnp.tile(jnp.array([-1.0, 1.0], F32), w // 2)[None, :]
    cg, sg = _angles(rows, cols, GQA_HEAD_DIM)
    cg = jnp.tile(lanes(cg), (1, 2))
    sg = jnp.tile(lanes(sg) * sign(GQA_HEAD_DIM), (1, 2))
    cm, sm = _angles(rows, cols, MLA_ROPE_DIM)
    cm, sm = lanes(cm), lanes(sm) * sign(MLA_ROPE_DIM)
    one64, z64, z32 = jnp.ones((L, 64), F32), jnp.zeros((L, 64), F32), jnp.zeros((L, 32), F32)
    cmq = jnp.concatenate([one64, cm, z32], 1) * SCALE_B
    smq = jnp.concatenate([z64, sm, z32], 1) * SCALE_B
    cmk = jnp.concatenate([z64, cm, z32], 1)
    smk = jnp.concatenate([z64, sm, z32], 1)
    return (cg, sg, cmq, smq, cmk, smk)


def _tile_sizes(n):
    tm = min(256, n)
    tq_gqa = min(128, n)
    tq_mla = min(1024, n)
    tk = min(512, n // KV_UNROLL)
    return tm, tq_gqa, tq_mla, tk


MOE_ROWS = 256


def _with_ones(vt, groups):
    lead, (rows, n) = vt.shape[:-2], vt.shape[-2:]
    g = vt.reshape(lead + (groups, rows // groups, n))
    ones = jnp.ones(lead + (groups, ONES_ROWS, n), vt.dtype)
    return jnp.concatenate([g, ones], axis=-2).reshape(lead + (rows + groups * ONES_ROWS, n))


def _trunk(x, pw, meta_kv, bias):
    B, n, D = x.shape
    T = B * n
    tm, tq_gqa, tq_mla, tk = _tile_sizes(n)
    t = jnp.arange(n, dtype=jnp.int32)
    tables = _rope_tables(t // GRID_W, t % GRID_W)
    x2d = x.reshape(T, D)
    qa, ka, va, qb, kb, vb = _project(x2d, tables, pw, tm)
    ka_m, vat_m, kb_m, vbt_m = meta_kv
    r3 = lambda a: a.reshape(B, n, a.shape[-1])
    t3 = lambda a: jnp.swapaxes(r3(a), 1, 2)
    oa = _gqa_attention(r3(qa), r3(ka), _with_ones(t3(va), 1), ka_m, vat_m, bias, tq_gqa, tk)
    ob = _mla_attention(r3(qb), r3(kb), _with_ones(t3(vb), MLA_HEADS), kb_m, vbt_m, bias,
                        tq_mla, tk)
    h, u, route = _post(oa.reshape(T, GQA_WIDTH), ob.reshape(T, MLA_WIDTH), x2d, pw, tm)
    eid = route[:, :2].astype(jnp.int32)
    y = _moe(u, eid, pw, MOE_ROWS)
    out = _final(h, y, route, pw["gfin"], tm)
    return out.reshape(B, n, D)


def kernel(x_prompt, x_sample, meta_tokens, norm_mix_g, w_in, gqa_q_norm_g, gqa_k_norm_g, mla_q_norm_g, mla_kv_norm_g, w_mla_q_up, w_mla_kv_up, gqa_out_norm_g, mla_out_norm_g, w_out, norm_ffn_g, w_router_group, b_router_group, w_router_expert, b_router_expert, w_expert_gate, w_expert_up, w_expert_down, norm_final_g):
    pw = _prep_weights(norm_mix_g[0], w_in[0], gqa_q_norm_g[0], gqa_k_norm_g[0],
                       mla_q_norm_g[0], mla_kv_norm_g[0], w_mla_q_up[0], w_mla_kv_up[0],
                       gqa_out_norm_g[0], mla_out_norm_g[0], w_out[0], norm_ffn_g[0],
                       w_router_group[0], b_router_group[0], w_router_expert[0],
                       b_router_expert[0], w_expert_gate[0], w_expert_up[0],
                       w_expert_down[0], norm_final_g)
    meta_x = jnp.concatenate([meta_tokens.astype(F32),
                              jnp.zeros((META_PAD - N_META, D_MODEL), F32)], axis=0)
    mpos = jnp.arange(META_PAD, dtype=jnp.int32)
    meta_tables = _rope_tables(jnp.full((META_PAD,), -1, jnp.int32),
                               jnp.where(mpos < N_META, mpos, 0))
    _, ka_m, va_m, _, kb_m, vb_m = _project(meta_x, meta_tables, pw, META_PAD)
    bias = jnp.where(mpos < N_META, 0.0, NEG_BIG).astype(F32)[:, None]
    meta_kv = (ka_m, _with_ones(va_m.T, 1), kb_m, _with_ones(vb_m.T, MLA_HEADS))
    return (_trunk(x_prompt, pw, meta_kv, bias), _trunk(x_sample, pw, meta_kv, bias))
```

```python
import functools
import math

import jax
import jax.numpy as jnp
from jax import lax
from jax.experimental import pallas as pl
from jax.experimental.pallas import tpu as pltpu

F32 = jnp.float32
BF16 = jnp.bfloat16

D_MODEL = 1024
N_META = 16
GRID_W = 64
ROPE_THETA = 10000.0
NORM_EPS = 1e-6

GQA_HEADS = 8
GQA_KV_HEADS = 2
GQA_HEAD_DIM = 64
GQA_WIDTH = GQA_HEADS * GQA_HEAD_DIM

MLA_HEADS = 8
MLA_Q_RANK = 256
MLA_KV_RANK = 128
MLA_NOPE_DIM = 64
MLA_ROPE_DIM = 32
MLA_V_DIM = 64
MLA_QK_DIM = MLA_NOPE_DIM + MLA_ROPE_DIM
MLA_WIDTH = MLA_HEADS * MLA_V_DIM

N_GROUPS = 4
EXPERTS_PER_GROUP = 8
N_EXPERTS = N_GROUPS * EXPERTS_PER_GROUP
D_EXPERT = 256

LANES = 128
HALF = LANES // 2
META_PAD = 128
ONES_ROWS = 16
MLA_VROWS = MLA_V_DIM + ONES_ROWS
KV_UNROLL = 8
NEG_BIG = -1e30
VMEM_LIMIT = 56 * 1024 * 1024

LOG2E = 1.4426950408889634
SCALE_A = GQA_HEAD_DIM ** -0.5 * LOG2E
SCALE_B = MLA_QK_DIM ** -0.5 * LOG2E

GQA_HEAD_ORDER = (0, 4, 1, 5, 2, 6, 3, 7)

_C_Q, _C_QS, _C_K, _C_KS, _C_V, _C_CQ, _C_CKV, _C_KR, _C_KRS, _C_END = (
    0, 512, 1024, 1152, 1280, 1408, 1664, 1792, 1920, 2048)

NT_DIMS = (((1,), (1,)), ((), ()))


def _cparams(sem):
    return pltpu.CompilerParams(dimension_semantics=sem, vmem_limit_bytes=VMEM_LIMIT)


def _full(shape):
    nd = len(shape)
    return pl.BlockSpec(shape, lambda *_: (0,) * nd)


def _rms(x, g):
    return x * lax.rsqrt(jnp.mean(x * x, axis=-1, keepdims=True) + NORM_EPS) * g


def _tile_lanes(a, reps):
    return jnp.concatenate([a] * reps, axis=1)


def _head_mean_sq(z, bd):
    sq = z * z
    hi = sq.astype(BF16)
    lo = (sq - hi.astype(F32)).astype(BF16)
    return (jnp.dot(hi, bd, preferred_element_type=F32)
            + jnp.dot(lo, bd, preferred_element_type=F32))


def _proj_kernel(x_ref, cg_ref, sg_ref, cmq_ref, smq_ref, cmk_ref, smk_ref,
                 gmix_ref, w1_ref, gq_ref, gqs_ref, gk_ref, gks_ref, gcq_ref, gckv_ref,
                 wqu_ref, wqus_ref, wku_ref, wvu_ref, bdq_ref, bdk_ref,
                 qa_ref, ka_ref, va_ref, qb_ref, kb_ref, vb_ref):
    u = _rms(x_ref[...], gmix_ref[...]).astype(BF16)
    z = jnp.dot(u, w1_ref[...], preferred_element_type=F32)

    cg, sg = cg_ref[...], sg_ref[...]
    zq, zqs = z[:, _C_Q:_C_QS], z[:, _C_QS:_C_K]
    rq = lax.rsqrt(_head_mean_sq(zq, bdq_ref[...]) + NORM_EPS) * SCALE_A
    c4, s4 = _tile_lanes(cg, 4), _tile_lanes(sg, 4)
    qa_ref[...] = (rq * (zq * gq_ref[...] * c4 + zqs * gqs_ref[...] * s4)).astype(BF16)
    zk, zks = z[:, _C_K:_C_KS], z[:, _C_KS:_C_V]
    rk = lax.rsqrt(_head_mean_sq(zk, bdk_ref[...]) + NORM_EPS)
    ka_ref[...] = (rk * (zk * gk_ref[...] * cg + zks * gks_ref[...] * sg)).astype(BF16)
    va_ref[...] = z[:, _C_V:_C_CQ].astype(BF16)

    cqn = _rms(z[:, _C_CQ:_C_CKV], gcq_ref[...]).astype(BF16)
    zqb = jnp.dot(cqn, wqu_ref[...], preferred_element_type=F32)
    zqbs = jnp.dot(cqn, wqus_ref[...], preferred_element_type=F32)
    qb_ref[...] = (zqb * _tile_lanes(cmq_ref[...], MLA_HEADS)
                   + zqbs * _tile_lanes(smq_ref[...], MLA_HEADS)).astype(BF16)
    ckvn = _rms(z[:, _C_CKV:_C_KR], gckv_ref[...]).astype(BF16)
    kn = jnp.dot(ckvn, wku_ref[...], preferred_element_type=F32)
    kr = z[:, _C_KR:_C_KRS] * cmk_ref[...] + z[:, _C_KRS:_C_END] * smk_ref[...]
    kb_ref[...] = (kn + _tile_lanes(kr, MLA_HEADS)).astype(BF16)
    vb_ref[...] = jnp.dot(ckvn, wvu_ref[...], preferred_element_type=F32).astype(BF16)


def _project(x2d, tables, pw, tm):
    T = x2d.shape[0]
    n_pos = tables[0].shape[0]
    pos_blocks = n_pos // tm
    row = lambda w: pl.BlockSpec((tm, w), lambda i: (i, 0))
    tab = pl.BlockSpec((tm, LANES), lambda i: (i % pos_blocks, 0))
    weights = (pw["gmix"], pw["w1"], pw["gq"], pw["gqs"], pw["gk"], pw["gks"], pw["gcq"],
               pw["gckv"], pw["wqu"], pw["wqus"], pw["wku"], pw["wvu"], pw["bdq"], pw["bdk"])
    widths = (GQA_WIDTH, LANES, LANES, MLA_HEADS * LANES, MLA_HEADS * LANES, MLA_WIDTH)
    return pl.pallas_call(
        _proj_kernel,
        grid=(T // tm,),
        in_specs=[row(D_MODEL)] + [tab] * 6 + [_full(w.shape) for w in weights],
        out_specs=[row(w) for w in widths],
        out_shape=[jax.ShapeDtypeStruct((T, w), BF16) for w in widths],
        compiler_params=_cparams(("parallel",)),
        name="proj",
    )(x2d, *tables, *weights)


def _flash(load_q, load_k, load_vt, km, vmt, bias, n_kv, tk, vdim, acc_ref, s_refs):
    assert n_kv >= KV_UNROLL and n_kv % KV_UNROLL == 0 and KV_UNROLL % 2 == 0

    def scores(j):
        off = pl.multiple_of(j * tk, tk)
        return lax.dot_general(load_k(off), load_q(), NT_DIMS, preferred_element_type=F32)

    def consume(s, j, m_prev):
        off = pl.multiple_of(j * tk, tk)
        m_new = jnp.maximum(m_prev, jnp.max(s, axis=0, keepdims=True))
        alpha = jnp.exp2(m_prev - m_new)
        p = jnp.exp2(s - m_new)
        acc_ref[...] = alpha * acc_ref[...] + jnp.dot(
            load_vt(off), p.astype(BF16), preferred_element_type=F32)
        return m_new

    s = lax.dot_general(km, load_q(), NT_DIMS, preferred_element_type=F32) + bias
    m0 = jnp.max(s, axis=0, keepdims=True)
    acc_ref[...] = jnp.dot(vmt, jnp.exp2(s - m0).astype(BF16), preferred_element_type=F32)
    s_refs[0][...] = scores(0)

    def group(i, m, last):
        j0 = KV_UNROLL * i
        for u in range(KV_UNROLL):
            if not (last and u == KV_UNROLL - 1):
                s_refs[(u + 1) % 2][...] = scores(j0 + u + 1)
            m = consume(s_refs[u % 2][...], j0 + u, m)
        return m

    trips = n_kv // KV_UNROLL
    m = lax.fori_loop(0, trips - 1, lambda i, c: group(i, c, False), m0)
    group(trips - 1, m, True)
    acc = acc_ref[...]
    return acc[:vdim] * (1.0 / acc[vdim:vdim + 1])


def _gqa_kernel(q_ref, k_ref, vt_ref, km_ref, vmt_ref, bias_ref, o_ref,
                qs_ref, acc_ref, s0_ref, s1_ref, *, tq, tk, n_kv):
    lo = lax.broadcasted_iota(jnp.int32, (tq, LANES), 1) < HALF
    zero = jnp.zeros((tq, LANES), BF16)
    for c in range(GQA_HEADS // 2):
        qc = q_ref[0, :, c * LANES:(c + 1) * LANES]
        qs_ref[(2 * c) * tq:(2 * c + 1) * tq, :] = jnp.where(lo, qc, zero)
        qs_ref[(2 * c + 1) * tq:(2 * c + 2) * tq, :] = jnp.where(lo, zero, qc)
    ot = _flash(lambda: qs_ref[...],
                lambda off: k_ref[0, pl.ds(off, tk), :],
                lambda off: vt_ref[0, :, pl.ds(off, tk)],
                km_ref[...], vmt_ref[...], bias_ref[...], n_kv, tk, LANES,
                acc_ref, (s0_ref, s1_ref))
    for c in range(GQA_HEADS // 2):
        pair = jnp.concatenate([ot[:HALF, (2 * c) * tq:(2 * c + 1) * tq],
                                ot[HALF:, (2 * c + 1) * tq:(2 * c + 2) * tq]], axis=0)
        o_ref[0, :, c * LANES:(c + 1) * LANES] = pair.T


def _gqa_attention(qa, ka, vat, km, vmt, bias, tq, tk):
    B, n, _ = qa.shape
    R = GQA_HEADS * tq
    kern = functools.partial(_gqa_kernel, tq=tq, tk=tk, n_kv=n // tk)
    return pl.pallas_call(
        kern,
        grid=(B, n // tq),
        in_specs=[pl.BlockSpec((1, tq, GQA_WIDTH), lambda b, i: (b, i, 0)),
                  pl.BlockSpec((1, n, LANES), lambda b, i: (b, 0, 0)),
                  pl.BlockSpec((1, LANES + ONES_ROWS, n), lambda b, i: (b, 0, 0)),
                  _full(km.shape), _full(vmt.shape), _full(bias.shape)],
        out_specs=pl.BlockSpec((1, tq, GQA_WIDTH), lambda b, i: (b, i, 0)),
        out_shape=jax.ShapeDtypeStruct((B, n, GQA_WIDTH), F32),
        scratch_shapes=[pltpu.VMEM((R, LANES), BF16), pltpu.VMEM((LANES + ONES_ROWS, R), F32),
                        pltpu.VMEM((tk, R), F32), pltpu.VMEM((tk, R), F32)],
        compiler_params=_cparams(("parallel", "arbitrary")),
        name="gqa_attn",
    )(qa, ka, vat, km, vmt, bias)


def _mla_kernel(q_ref, k_ref, vt_ref, km_ref, vmt_ref, bias_ref, o_ref, acc_ref,
                s0_ref, s1_ref, *, tq, tk, n_kv):
    outs = []
    for h in range(2):
        sl = slice(h * LANES, (h + 1) * LANES)
        vs = slice(h * MLA_VROWS, (h + 1) * MLA_VROWS)
        outs.append(_flash(lambda: q_ref[0, :, sl],
                           lambda off: k_ref[0, pl.ds(off, tk), sl],
                           lambda off: vt_ref[0, vs, pl.ds(off, tk)],
                           km_ref[:, sl], vmt_ref[vs, :], bias_ref[...], n_kv, tk,
                           MLA_V_DIM, acc_ref, (s0_ref, s1_ref)))
    o_ref[0] = jnp.concatenate(outs, axis=0).T


def _mla_attention(qb, kb, vbt, km, vmt, bias, tq, tk):
    B, n, _ = qb.shape
    pairs = MLA_HEADS // 2
    kern = functools.partial(_mla_kernel, tq=tq, tk=tk, n_kv=n // tk)
    return pl.pallas_call(
        kern,
        grid=(B, pairs, n // tq),
        in_specs=[pl.BlockSpec((1, tq, 2 * LANES), lambda b, j, i: (b, i, j)),
                  pl.BlockSpec((1, n, 2 * LANES), lambda b, j, i: (b, 0, j)),
                  pl.BlockSpec((1, 2 * MLA_VROWS, n), lambda b, j, i: (b, j, 0)),
                  pl.BlockSpec((META_PAD, 2 * LANES), lambda b, j, i: (0, j)),
                  pl.BlockSpec((2 * MLA_VROWS, META_PAD), lambda b, j, i: (j, 0)),
                  _full(bias.shape)],
        out_specs=pl.BlockSpec((1, tq, LANES), lambda b, j, i: (b, i, j)),
        out_shape=jax.ShapeDtypeStruct((B, n, MLA_WIDTH), F32),
        scratch_shapes=[pltpu.VMEM((MLA_VROWS, tq), F32), pltpu.VMEM((tk, tq), F32),
                        pltpu.VMEM((tk, tq), F32)],
        compiler_params=_cparams(("parallel", "parallel", "arbitrary")),
        name="mla_attn",
    )(qb, kb, vbt, km, vmt, bias)


def _post_kernel(oa_ref, ob_ref, x_ref, ga_ref, gb_ref, wout_ref, gffn_ref,
                 wrh_ref, wrl_ref, br_ref, h_ref, u_ref, route_ref):
    o = jnp.concatenate([_rms(oa_ref[...], ga_ref[...]), _rms(ob_ref[...], gb_ref[...])],
                        axis=1).astype(BF16)
    h = x_ref[...] + jnp.dot(o, wout_ref[...], preferred_element_type=F32)
    h_ref[...] = h
    u = _rms(h, gffn_ref[...])
    u_ref[...] = u

    u_hi = u.astype(BF16)
    u_lo = (u - u_hi.astype(F32)).astype(BF16)
    wrh = wrh_ref[...]
    logits = (jnp.dot(u_hi, wrh, preferred_element_type=F32)
              + jnp.dot(u_lo, wrh, preferred_element_type=F32)
              + jnp.dot(u_hi, wrl_ref[...], preferred_element_type=F32)) + br_ref[...]

    lane = lax.broadcasted_iota(jnp.int32, logits.shape, 1)
    far = jnp.int32(LANES)
    neg = jnp.float32(-jnp.inf)
    lg = jnp.where(lane < N_GROUPS, logits, neg)
    mg = jnp.max(lg, axis=1, keepdims=True)
    p_grp = 1.0 / jnp.sum(jnp.exp(lg - mg), axis=1, keepdims=True)
    grp = jnp.min(jnp.where(lg == mg, lane, far), axis=1, keepdims=True)
    e_lo = N_GROUPS + grp * EXPERTS_PER_GROUP
    le = jnp.where((lane >= e_lo) & (lane < e_lo + EXPERTS_PER_GROUP), logits, neg)
    m1 = jnp.max(le, axis=1, keepdims=True)
    i1 = jnp.min(jnp.where(le == m1, lane, far), axis=1, keepdims=True)
    le2 = jnp.where(lane == i1, neg, le)
    m2 = jnp.max(le2, axis=1, keepdims=True)
    i2 = jnp.min(jnp.where(le2 == m2, lane, far), axis=1, keepdims=True)
    t = jnp.exp(m2 - m1)
    g1 = p_grp / (1.0 + t)
    g2 = g1 * t
    e1 = (i1 - N_GROUPS).astype(F32)
    e2 = (i2 - N_GROUPS).astype(F32)
    route_ref[...] = jnp.where(lane == 0, e1, jnp.where(lane == 1, e2, jnp.where(
        lane == 2, g1, jnp.where(lane == 3, g2, 0.0))))


def _post(oa, ob, x2d, pw, tm):
    T = x2d.shape[0]
    row = lambda w: pl.BlockSpec((tm, w), lambda i: (i, 0))
    weights = (pw["ga"], pw["gb"], pw["wout"], pw["gffn"], pw["wrh"], pw["wrl"], pw["br"])
    return pl.pallas_call(
        _post_kernel,
        grid=(T // tm,),
        in_specs=[row(GQA_WIDTH), row(MLA_WIDTH), row(D_MODEL)]
        + [_full(w.shape) for w in weights],
        out_specs=[row(D_MODEL), row(D_MODEL), row(LANES)],
        out_shape=[jax.ShapeDtypeStruct((T, D_MODEL), F32),
                   jax.ShapeDtypeStruct((T, D_MODEL), F32),
                   jax.ShapeDtypeStruct((T, LANES), F32)],
        compiler_params=_cparams(("parallel",)),
        name="post",
    )(oa, ob, x2d, *weights)


def _moe_kernel(be_ref, nused_ref, tok_ref, tokn_ref, dstp_ref, dst_ref, x_hbm, wgu_ref, wd_ref,
                y_hbm, xbuf, ybuf, gsem, ssem, *, blk):
    i = pl.program_id(0)
    nb = pl.num_programs(0)
    slot = i % 2
    n_used = nused_ref[0]

    def gather_copy(t, r, s):
        return pltpu.make_async_copy(x_hbm.at[pl.ds(t, 1), :], xbuf.at[s, pl.ds(r, 1), :],
                                     gsem.at[s])

    def scatter_copy(d, r, s):
        return pltpu.make_async_copy(ybuf.at[s, pl.ds(r, 1), :], y_hbm.at[pl.ds(d, 1), :],
                                     ssem.at[s])

    def start_rows(make, idx_ref, s, unrolled):
        if unrolled:
            for r in range(blk):
                make(idx_ref[0, 0, r], r, s).start(priority=r % 2)
        else:
            def body(r, c):
                make(idx_ref[0, 0, r], r, s).start()
                return c
            lax.fori_loop(0, blk, body, 0)

    def wait_rows(make, s):
        def body(r, c):
            make(0, r, s).wait()
            return c
        lax.fori_loop(0, blk, body, 0, unroll=8)

    def experts(s):
        xe = xbuf[s].astype(BF16)
        gu = jnp.dot(xe, wgu_ref[0], preferred_element_type=F32)
        g, up = gu[:, :D_EXPERT], gu[:, D_EXPERT:]
        hid = (g * jax.nn.sigmoid(g) * up).astype(BF16)
        ybuf[s] = jnp.dot(hid, wd_ref[0], preferred_element_type=F32)

    @pl.when(i == 0)
    def _():
        start_rows(gather_copy, tok_ref, 0, False)

    wait_rows(gather_copy, slot)

    @pl.when(i >= 2)
    def _():
        wait_rows(scatter_copy, slot)

    @pl.when(i == 0)
    def _():
        start_rows(gather_copy, tokn_ref, 1 - slot, False)
        experts(slot)

    for par in range(2):
        @pl.when((i >= 1) & (i < n_used) & (slot == par))
        def _():
            start_rows(gather_copy, tokn_ref, 1 - par, True)
            experts(par)
            start_rows(scatter_copy, dstp_ref, 1 - par, True)

    @pl.when((i >= 1) & (i >= n_used))
    def _():
        start_rows(gather_copy, tokn_ref, 1 - slot, False)
        ybuf[slot] = jnp.zeros((blk, D_MODEL), F32)
        start_rows(scatter_copy, dstp_ref, 1 - slot, False)

    @pl.when(i == nb - 1)
    def _():
        start_rows(scatter_copy, dst_ref, slot, False)
        wait_rows(gather_copy, 1 - slot)

        @pl.when(i >= 1)
        def _():
            wait_rows(scatter_copy, 1 - slot)
        wait_rows(scatter_copy, slot)


def _moe_plan(eid, blk):
    T = eid.shape[0]
    A = 2 * T
    n_blocks = (A + N_EXPERTS * (blk - 1) + blk - 1) // blk
    P = n_blocks * blk
    e_flat = eid.T.reshape(A)
    onehot = (e_flat[:, None] == jnp.arange(N_EXPERTS, dtype=jnp.int32)[None, :]).astype(jnp.int32)
    csum = jnp.cumsum(onehot, axis=0)
    rank = jnp.sum(csum * onehot, axis=1) - 1
    counts = csum[-1]
    padded = (counts + blk - 1) // blk * blk
    pends = jnp.cumsum(padded)
    pstarts = pends - padded
    slot = jnp.sum(onehot * pstarts[None, :], axis=1) + rank
    a_idx = jnp.arange(A, dtype=jnp.int32)
    buf_a = jnp.full((P,), -1, jnp.int32).at[slot].set(a_idx)
    invalid = buf_a < 0
    buf_tok = jnp.where(invalid, 0, buf_a % T)
    buf_dst = jnp.where(invalid, A + jnp.cumsum(invalid.astype(jnp.int32)) - 1, buf_a)
    block_expert = jnp.minimum(
        jnp.searchsorted(pends, jnp.arange(n_blocks, dtype=jnp.int32) * blk, side="right"),
        N_EXPERTS - 1).astype(jnp.int32)
    n_used = (pends[-1] // blk).astype(jnp.int32).reshape(1)
    return block_expert, n_used, buf_tok, buf_dst, n_blocks


def _moe(u2d, eid, pw, blk):
    block_expert, n_used, buf_tok, buf_dst, n_blocks = _moe_plan(eid, blk)
    P = n_blocks * blk
    tok3 = buf_tok.reshape(n_blocks, 1, blk)
    dst3 = buf_dst.reshape(n_blocks, 1, blk)
    last = n_blocks - 1
    idx_spec = lambda f: pl.BlockSpec((1, 1, blk), f, memory_space=pltpu.SMEM)
    grid_spec = pltpu.PrefetchScalarGridSpec(
        num_scalar_prefetch=2,
        grid=(n_blocks,),
        in_specs=[idx_spec(lambda i, be, nu: (i, 0, 0)),
                  idx_spec(lambda i, be, nu: (jnp.minimum(i + 1, last), 0, 0)),
                  idx_spec(lambda i, be, nu: (jnp.maximum(i - 1, 0), 0, 0)),
                  idx_spec(lambda i, be, nu: (i, 0, 0)),
                  pl.BlockSpec(memory_space=pl.ANY),
                  pl.BlockSpec((1, D_MODEL, 2 * D_EXPERT), lambda i, be, nu: (be[i], 0, 0)),
                  pl.BlockSpec((1, D_EXPERT, D_MODEL), lambda i, be, nu: (be[i], 0, 0))],
        out_specs=pl.BlockSpec(memory_space=pl.ANY),
        scratch_shapes=[pltpu.VMEM((2, blk, D_MODEL), F32), pltpu.VMEM((2, blk, D_MODEL), F32),
                        pltpu.SemaphoreType.DMA((2,)), pltpu.SemaphoreType.DMA((2,))],
    )
    return pl.pallas_call(
        functools.partial(_moe_kernel, blk=blk),
        grid_spec=grid_spec,
        out_shape=jax.ShapeDtypeStruct((P, D_MODEL), F32),
        compiler_params=_cparams(("arbitrary",)),
        name="moe",
    )(block_expert, n_used, tok3, tok3, dst3, dst3, u2d, pw["wgu"], pw["wd"])


def _final_kernel(h_ref, y1_ref, y2_ref, route_ref, g_ref, o_ref):
    r = route_ref[...]
    y = y1_ref[...] * r[:, 2:3] + y2_ref[...] * r[:, 3:4]
    o_ref[...] = _rms(h_ref[...] + y, g_ref[...])


def _final(h2d, y, route, gfin, tm):
    T = h2d.shape[0]
    second = T // tm
    return pl.pallas_call(
        _final_kernel,
        grid=(T // tm,),
        in_specs=[pl.BlockSpec((tm, D_MODEL), lambda i: (i, 0)),
                  pl.BlockSpec((tm, D_MODEL), lambda i: (i, 0)),
                  pl.BlockSpec((tm, D_MODEL), lambda i: (second + i, 0)),
                  pl.BlockSpec((tm, LANES), lambda i: (i, 0)),
                  _full(gfin.shape)],
        out_specs=pl.BlockSpec((tm, D_MODEL), lambda i: (i, 0)),
        out_shape=jax.ShapeDtypeStruct((T, D_MODEL), F32),
        compiler_params=_cparams(("parallel",)),
        name="final",
    )(h2d, y, y, route, gfin)


def _pair_swap(w):
    s = w.shape
    return w.reshape(s[:-1] + (s[-1] // 2, 2))[..., ::-1].reshape(s)


def _prep_weights(norm_mix_g, w_in, gqa_q_norm_g, gqa_k_norm_g, mla_q_norm_g, mla_kv_norm_g,
                  w_mla_q_up, w_mla_kv_up, gqa_out_norm_g, mla_out_norm_g, w_out, norm_ffn_g,
                  w_router_group, b_router_group, w_router_expert, b_router_expert,
                  w_expert_gate, w_expert_up, w_expert_down, norm_final_g):
    D = D_MODEL
    order = jnp.array(GQA_HEAD_ORDER)
    wq = w_in[:, :512].reshape(D, GQA_HEADS, GQA_HEAD_DIM)[:, order].reshape(D, GQA_WIDTH)
    wk = w_in[:, 512:640]
    wkr = w_in[:, 1152:1184]
    z64 = jnp.zeros((D, 64), F32)
    z32 = jnp.zeros((D, 32), F32)
    w1 = jnp.concatenate([
        wq, _pair_swap(wq), wk, _pair_swap(wk), w_in[:, 640:768], w_in[:, 768:1024],
        w_in[:, 1024:1152], jnp.concatenate([z64, wkr, z32], 1),
        jnp.concatenate([z64, _pair_swap(wkr), z32], 1)], axis=1).astype(BF16)

    qu = w_mla_q_up.reshape(MLA_Q_RANK, MLA_HEADS, MLA_QK_DIM)
    zq32 = jnp.zeros((MLA_Q_RANK, MLA_HEADS, 32), F32)
    zq64 = jnp.zeros((MLA_Q_RANK, MLA_HEADS, 64), F32)
    wqu = jnp.concatenate([qu, zq32], axis=2).reshape(MLA_Q_RANK, MLA_HEADS * LANES)
    wqus = jnp.concatenate([zq64, _pair_swap(qu[:, :, MLA_NOPE_DIM:]), zq32],
                           axis=2).reshape(MLA_Q_RANK, MLA_HEADS * LANES)
    kvu = w_mla_kv_up.reshape(MLA_KV_RANK, MLA_HEADS, MLA_NOPE_DIM + MLA_V_DIM)
    wku = jnp.concatenate([kvu[:, :, :MLA_NOPE_DIM],
                           jnp.zeros((MLA_KV_RANK, MLA_HEADS, 64), F32)],
                          axis=2).reshape(MLA_KV_RANK, MLA_HEADS * LANES)
    wvu = kvu[:, :, MLA_NOPE_DIM:].reshape(MLA_KV_RANK, MLA_WIDTH)

    head_block = jnp.full((GQA_HEAD_DIM, GQA_HEAD_DIM), 1.0 / GQA_HEAD_DIM, F32)
    perm_rows = lambda a: a.reshape((GQA_HEADS, GQA_HEAD_DIM) + a.shape[1:])[order].reshape(a.shape)
    wr = jnp.concatenate([w_router_group, w_router_expert,
                          jnp.zeros((D, LANES - N_GROUPS - N_EXPERTS), F32)], axis=1)
    wrh = wr.astype(BF16)
    br = jnp.concatenate([b_router_group, b_router_expert,
                          jnp.zeros((LANES - N_GROUPS - N_EXPERTS,), F32)])[None, :]
    return dict(
        gmix=norm_mix_g[None, :], w1=w1,
        gq=jnp.tile(gqa_q_norm_g, GQA_HEADS)[None, :],
        gqs=jnp.tile(_pair_swap(gqa_q_norm_g), GQA_HEADS)[None, :],
        gk=jnp.tile(gqa_k_norm_g, GQA_KV_HEADS)[None, :],
        gks=jnp.tile(_pair_swap(gqa_k_norm_g), GQA_KV_HEADS)[None, :],
        gcq=mla_q_norm_g[None, :], gckv=mla_kv_norm_g[None, :],
        wqu=wqu.astype(BF16), wqus=wqus.astype(BF16), wku=wku.astype(BF16),
        wvu=wvu.astype(BF16),
        bdq=jnp.kron(jnp.eye(GQA_HEADS, dtype=F32), head_block).astype(BF16),
        bdk=jnp.kron(jnp.eye(GQA_KV_HEADS, dtype=F32), head_block).astype(BF16),
        ga=perm_rows(gqa_out_norm_g)[None, :], gb=mla_out_norm_g[None, :],
        wout=jnp.concatenate([perm_rows(w_out[:GQA_WIDTH]), w_out[GQA_WIDTH:]], 0).astype(BF16),
        gffn=norm_ffn_g[None, :], wrh=wrh, wrl=(wr - wrh.astype(F32)).astype(BF16), br=br,
        wgu=jnp.concatenate([w_expert_gate, w_expert_up], axis=2).astype(BF16),
        wd=w_expert_down.astype(BF16), gfin=norm_final_g[None, :])


def _angles(rows, cols, rot_dim):
    half = rot_dim // 2
    inv = ROPE_THETA ** (-jnp.arange(0, half, 2, dtype=F32) / half)
    ang = jnp.concatenate([rows.astype(F32)[:, None] * inv,
                           cols.astype(F32)[:, None] * inv], axis=-1)
    return jnp.cos(ang), jnp.sin(ang)


def _rope_tables(rows, cols):
    L = rows.shape[0]
    lanes = lambda a: jnp.repeat(a, 2, axis=-1)
    sign = lambda w: jnp.tile(jnp.array([-1.0, 1.0], F32), w // 2)[None, :]
    cg, sg = _angles(rows, cols, GQA_HEAD_DIM)
    cg = jnp.tile(lanes(cg), (1, 2))
    sg = jnp.tile(lanes(sg) * sign(GQA_HEAD_DIM), (1, 2))
    cm, sm = _angles(rows, cols, MLA_ROPE_DIM)
    cm, sm = lanes(cm), lanes(sm) * sign(MLA_ROPE_DIM)
    one64, z64, z32 = jnp.ones((L, 64), F32), jnp.zeros((L, 64), F32), jnp.zeros((L, 32), F32)
    cmq = jnp.concatenate([one64, cm, z32], 1) * SCALE_B
    smq = jnp.concatenate([z64, sm, z32], 1) * SCALE_B
    cmk = jnp.concatenate([z64, cm, z32], 1)
    smk = jnp.concatenate([z64, sm, z32], 1)
    return (cg, sg, cmq, smq, cmk, smk)


def _tile_sizes(n):
    tm = min(256, n)
    tq_gqa = min(128, n)
    tq_mla = min(1024, n)
    tk = min(512, n // KV_UNROLL)
    return tm, tq_gqa, tq_mla, tk


MOE_ROWS = 256


def _with_ones(vt, groups):
    lead, (rows, n) = vt.shape[:-2], vt.shape[-2:]
    g = vt.reshape(lead + (groups, rows // groups, n))
    ones = jnp.ones(lead + (groups, ONES_ROWS, n), vt.dtype)
    return jnp.concatenate([g, ones], axis=-2).reshape(lead + (rows + groups * ONES_ROWS, n))


def _trunk(x, pw, meta_kv, bias):
    B, n, D = x.shape
    T = B * n
    tm, tq_gqa, tq_mla, tk = _tile_sizes(n)
    t = jnp.arange(n, dtype=jnp.int32)
    tables = _rope_tables(t // GRID_W, t % GRID_W)
    x2d = x.reshape(T, D)
    qa, ka, va, qb, kb, vb = _project(x2d, tables, pw, tm)
    ka_m, vat_m, kb_m, vbt_m = meta_kv
    r3 = lambda a: a.reshape(B, n, a.shape[-1])
    t3 = lambda a: jnp.swapaxes(r3(a), 1, 2)
    oa = _gqa_attention(r3(qa), r3(ka), _with_ones(t3(va), 1), ka_m, vat_m, bias, tq_gqa, tk)
    ob = _mla_attention(r3(qb), r3(kb), _with_ones(t3(vb), MLA_HEADS), kb_m, vbt_m, bias,
                        tq_mla, tk)
    h, u, route = _post(oa.reshape(T, GQA_WIDTH), ob.reshape(T, MLA_WIDTH), x2d, pw, tm)
    eid = route[:, :2].astype(jnp.int32)
    y = _moe(u, eid, pw, MOE_ROWS)
    out = _final(h, y, route, pw["gfin"], tm)
    return out.reshape(B, n, D)


def kernel(x_prompt, x_sample, meta_tokens, norm_mix_g, w_in, gqa_q_norm_g, gqa_k_norm_g, mla_q_norm_g, mla_kv_norm_g, w_mla_q_up, w_mla_kv_up, gqa_out_norm_g, mla_out_norm_g, w_out, norm_ffn_g, w_router_group, b_router_group, w_router_expert, b_router_expert, w_expert_gate, w_expert_up, w_expert_down, norm_final_g):
    pw = _prep_weights(norm_mix_g[0], w_in[0], gqa_q_norm_g[0], gqa_k_norm_g[0],
                       mla_q_norm_g[0], mla_kv_norm_g[0], w_mla_q_up[0], w_mla_kv_up[0],
                       gqa_out_norm_g[0], mla_out_norm_g[0], w_out[0], norm_ffn_g[0],
                       w_router_group[0], b_router_group[0], w_router_expert[0],
                       b_router_expert[0], w_expert_gate[0], w_expert_up[0],
                       w_expert_down[0], norm_final_g)
    meta_x = jnp.concatenate([meta_tokens.astype(F32),
                              jnp.zeros((META_PAD - N_META, D_MODEL), F32)], axis=0)
    mpos = jnp.arange(META_PAD, dtype=jnp.int32)
    meta_tables = _rope_tables(jnp.full((META_PAD,), -1, jnp.int32),
                               jnp.where(mpos < N_META, mpos, 0))
    _, ka_m, va_m, _, kb_m, vb_m = _project(meta_x, meta_tables, pw, META_PAD)
    bias = jnp.where(mpos < N_META, 0.0, NEG_BIG).astype(F32)[:, None]
    meta_kv = (ka_m, _with_ones(va_m.T, 1), kb_m, _with_ones(vb_m.T, MLA_HEADS))
    return (_trunk(x_prompt, pw, meta_kv, bias), _trunk(x_sample, pw, meta_kv, bias))
```

```python
import functools
import math

import jax
import jax.numpy as jnp
from jax import lax
from jax.experimental import pallas as pl
from jax.experimental.pallas import tpu as pltpu

F32 = jnp.float32
BF16 = jnp.bfloat16

D_MODEL = 1024
N_META = 16
GRID_W = 64
ROPE_THETA = 10000.0
NORM_EPS = 1e-6

GQA_HEADS = 8
GQA_KV_HEADS = 2
GQA_HEAD_DIM = 64
GQA_WIDTH = GQA_HEADS * GQA_HEAD_DIM

MLA_HEADS = 8
MLA_Q_RANK = 256
MLA_KV_RANK = 128
MLA_NOPE_DIM = 64
MLA_ROPE_DIM = 32
MLA_V_DIM = 64
MLA_QK_DIM = MLA_NOPE_DIM + MLA_ROPE_DIM
MLA_WIDTH = MLA_HEADS * MLA_V_DIM

N_GROUPS = 4
EXPERTS_PER_GROUP = 8
N_EXPERTS = N_GROUPS * EXPERTS_PER_GROUP
D_EXPERT = 256

LANES = 128
HALF = LANES // 2
META_PAD = 128
ONES_ROWS = 16
MLA_VROWS = MLA_V_DIM + ONES_ROWS
KV_UNROLL = 8
NEG_BIG = -1e30
VMEM_LIMIT = 56 * 1024 * 1024

LOG2E = 1.4426950408889634
SCALE_A = GQA_HEAD_DIM ** -0.5 * LOG2E
SCALE_B = MLA_QK_DIM ** -0.5 * LOG2E

GQA_HEAD_ORDER = (0, 4, 1, 5, 2, 6, 3, 7)

_C_Q, _C_QS, _C_K, _C_KS, _C_V, _C_CQ, _C_CKV, _C_KR, _C_KRS, _C_END = (
    0, 512, 1024, 1152, 1280, 1408, 1664, 1792, 1920, 2048)

NT_DIMS = (((1,), (1,)), ((), ()))


def _cparams(sem):
    return pltpu.CompilerParams(dimension_semantics=sem, vmem_limit_bytes=VMEM_LIMIT)


def _full(shape):
    nd = len(shape)
    return pl.BlockSpec(shape, lambda *_: (0,) * nd)


def _rms(x, g):
    return x * lax.rsqrt(jnp.mean(x * x, axis=-1, keepdims=True) + NORM_EPS) * g


def _tile_lanes(a, reps):
    return jnp.concatenate([a] * reps, axis=1)


def _head_mean_sq(z, bd):
    sq = z * z
    hi = sq.astype(BF16)
    lo = (sq - hi.astype(F32)).astype(BF16)
    return (jnp.dot(hi, bd, preferred_element_type=F32)
            + jnp.dot(lo, bd, preferred_element_type=F32))


def _proj_kernel(x_ref, cg_ref, sg_ref, cmq_ref, smq_ref, cmk_ref, smk_ref,
                 gmix_ref, w1_ref, gq_ref, gqs_ref, gk_ref, gks_ref, gcq_ref, gckv_ref,
                 wqu_ref, wqus_ref, wku_ref, wvu_ref, bdq_ref, bdk_ref,
                 qa_ref, ka_ref, va_ref, qb_ref, kb_ref, vb_ref):
    u = _rms(x_ref[...], gmix_ref[...]).astype(BF16)
    z = jnp.dot(u, w1_ref[...], preferred_element_type=F32)

    cg, sg = cg_ref[...], sg_ref[...]
    zq, zqs = z[:, _C_Q:_C_QS], z[:, _C_QS:_C_K]
    rq = lax.rsqrt(_head_mean_sq(zq, bdq_ref[...]) + NORM_EPS) * SCALE_A
    c4, s4 = _tile_lanes(cg, 4), _tile_lanes(sg, 4)
    qa_ref[...] = (rq * (zq * gq_ref[...] * c4 + zqs * gqs_ref[...] * s4)).astype(BF16)
    zk, zks = z[:, _C_K:_C_KS], z[:, _C_KS:_C_V]
    rk = lax.rsqrt(_head_mean_sq(zk, bdk_ref[...]) + NORM_EPS)
    ka_ref[...] = (rk * (zk * gk_ref[...] * cg + zks * gks_ref[...] * sg)).astype(BF16)
    va_ref[...] = z[:, _C_V:_C_CQ].astype(BF16)

    cqn = _rms(z[:, _C_CQ:_C_CKV], gcq_ref[...]).astype(BF16)
    zqb = jnp.dot(cqn, wqu_ref[...], preferred_element_type=F32)
    zqbs = jnp.dot(cqn, wqus_ref[...], preferred_element_type=F32)
    qb_ref[...] = (zqb * _tile_lanes(cmq_ref[...], MLA_HEADS)
                   + zqbs * _tile_lanes(smq_ref[...], MLA_HEADS)).astype(BF16)
    ckvn = _rms(z[:, _C_CKV:_C_KR], gckv_ref[...]).astype(BF16)
    kn = jnp.dot(ckvn, wku_ref[...], preferred_element_type=F32)
    kr = z[:, _C_KR:_C_KRS] * cmk_ref[...] + z[:, _C_KRS:_C_END] * smk_ref[...]
    kb_ref[...] = (kn + _tile_lanes(kr, MLA_HEADS)).astype(BF16)
    vb_ref[...] = jnp.dot(ckvn, wvu_ref[...], preferred_element_type=F32).astype(BF16)


def _project(x2d, tables, pw, tm):
    T = x2d.shape[0]
    n_pos = tables[0].shape[0]
    pos_blocks = n_pos // tm
    row = lambda w: pl.BlockSpec((tm, w), lambda i: (i, 0))
    tab = pl.BlockSpec((tm, LANES), lambda i: (i % pos_blocks, 0))
    weights = (pw["gmix"], pw["w1"], pw["gq"], pw["gqs"], pw["gk"], pw["gks"], pw["gcq"],
               pw["gckv"], pw["wqu"], pw["wqus"], pw["wku"], pw["wvu"], pw["bdq"], pw["bdk"])
    widths = (GQA_WIDTH, LANES, LANES, MLA_HEADS * LANES, MLA_HEADS * LANES, MLA_WIDTH)
    return pl.pallas_call(
        _proj_kernel,
        grid=(T // tm,),
        in_specs=[row(D_MODEL)] + [tab] * 6 + [_full(w.shape) for w in weights],
        out_specs=[row(w) for w in widths],
        out_shape=[jax.ShapeDtypeStruct((T, w), BF16) for w in widths],
        compiler_params=_cparams(("parallel",)),
        name="proj",
    )(x2d, *tables, *weights)


def _flash(load_q, load_k, load_vt, km, vmt, bias, n_kv, tk, vdim, acc_ref, s_refs):
    assert n_kv >= KV_UNROLL and n_kv % KV_UNROLL == 0 and KV_UNROLL % 2 == 0

    def scores(j):
        off = pl.multiple_of(j * tk, tk)
        return lax.dot_general(load_k(off), load_q(), NT_DIMS, preferred_element_type=F32)

    def consume(s, j, m_prev):
        off = pl.multiple_of(j * tk, tk)
        m_new = jnp.maximum(m_prev, jnp.max(s, axis=0, keepdims=True))
        alpha = jnp.exp2(m_prev - m_new)
        p = jnp.exp2(s - m_new)
        acc_ref[...] = alpha * acc_ref[...] + jnp.dot(
            load_vt(off), p.astype(BF16), preferred_element_type=F32)
        return m_new

    s = lax.dot_general(km, load_q(), NT_DIMS, preferred_element_type=F32) + bias
    m0 = jnp.max(s, axis=0, keepdims=True)
    acc_ref[...] = jnp.dot(vmt, jnp.exp2(s - m0).astype(BF16), preferred_element_type=F32)
    s_refs[0][...] = scores(0)

    def group(i, m, last):
        j0 = KV_UNROLL * i
        for u in range(KV_UNROLL):
            if not (last and u == KV_UNROLL - 1):
                s_refs[(u + 1) % 2][...] = scores(j0 + u + 1)
            m = consume(s_refs[u % 2][...], j0 + u, m)
        return m

    trips = n_kv // KV_UNROLL
    m = lax.fori_loop(0, trips - 1, lambda i, c: group(i, c, False), m0)
    group(trips - 1, m, True)
    acc = acc_ref[...]
    return acc[:vdim] * (1.0 / acc[vdim:vdim + 1])


def _gqa_kernel(q_ref, k_ref, vt_ref, km_ref, vmt_ref, bias_ref, o_ref,
                qs_ref, acc_ref, s0_ref, s1_ref, *, tq, tk, n_kv, nsub):
    lo = lax.broadcasted_iota(jnp.int32, (tq, LANES), 1) < HALF
    zero = jnp.zeros((tq, LANES), BF16)

    def sub_tile(i, carry):
        rows = pl.ds(pl.multiple_of(i * tq, tq), tq)
        for c in range(GQA_HEADS // 2):
            qc = q_ref[0, rows, c * LANES:(c + 1) * LANES]
            qs_ref[(2 * c) * tq:(2 * c + 1) * tq, :] = jnp.where(lo, qc, zero)
            qs_ref[(2 * c + 1) * tq:(2 * c + 2) * tq, :] = jnp.where(lo, zero, qc)
        ot = _flash(lambda: qs_ref[...],
                    lambda off: k_ref[0, pl.ds(off, tk), :],
                    lambda off: vt_ref[0, :, pl.ds(off, tk)],
                    km_ref[...], vmt_ref[...], bias_ref[...], n_kv, tk, LANES,
                    acc_ref, (s0_ref, s1_ref))
        for c in range(GQA_HEADS // 2):
            pair = jnp.concatenate([ot[:HALF, (2 * c) * tq:(2 * c + 1) * tq],
                                    ot[HALF:, (2 * c + 1) * tq:(2 * c + 2) * tq]], axis=0)
            o_ref[0, rows, c * LANES:(c + 1) * LANES] = pair.T
        return carry

    lax.fori_loop(0, nsub, sub_tile, 0)


def _gqa_attention(qa, ka, vat, km, vmt, bias, tq, tk, nsub):
    B, n, _ = qa.shape
    R = GQA_HEADS * tq
    tb = tq * nsub
    kern = functools.partial(_gqa_kernel, tq=tq, tk=tk, n_kv=n // tk, nsub=nsub)
    return pl.pallas_call(
        kern,
        grid=(B, n // tb),
        in_specs=[pl.BlockSpec((1, tb, GQA_WIDTH), lambda b, i: (b, i, 0)),
                  pl.BlockSpec((1, n, LANES), lambda b, i: (b, 0, 0)),
                  pl.BlockSpec((1, LANES + ONES_ROWS, n), lambda b, i: (b, 0, 0)),
                  _full(km.shape), _full(vmt.shape), _full(bias.shape)],
        out_specs=pl.BlockSpec((1, tb, GQA_WIDTH), lambda b, i: (b, i, 0)),
        out_shape=jax.ShapeDtypeStruct((B, n, GQA_WIDTH), F32),
        scratch_shapes=[pltpu.VMEM((R, LANES), BF16), pltpu.VMEM((LANES + ONES_ROWS, R), F32),
                        pltpu.VMEM((tk, R), F32), pltpu.VMEM((tk, R), F32)],
        compiler_params=_cparams(("parallel", "arbitrary")),
        name="gqa_attn",
    )(qa, ka, vat, km, vmt, bias)


def _mla_kernel(q_ref, k_ref, vt_ref, km_ref, vmt_ref, bias_ref, o_ref, acc_ref,
                s0_ref, s1_ref, *, tq, tk, n_kv, nsub):
    def sub_tile(i, carry):
        rows = pl.ds(pl.multiple_of(i * tq, tq), tq)
        outs = []
        for h in range(2):
            sl = slice(h * LANES, (h + 1) * LANES)
            vs = slice(h * MLA_VROWS, (h + 1) * MLA_VROWS)
            outs.append(_flash(lambda: q_ref[0, rows, sl],
                               lambda off: k_ref[0, pl.ds(off, tk), sl],
                               lambda off: vt_ref[0, vs, pl.ds(off, tk)],
                               km_ref[:, sl], vmt_ref[vs, :], bias_ref[...], n_kv, tk,
                               MLA_V_DIM, acc_ref, (s0_ref, s1_ref)))
        o_ref[0, rows, :] = jnp.concatenate(outs, axis=0).T
        return carry

    lax.fori_loop(0, nsub, sub_tile, 0)


def _mla_attention(qb, kb, vbt, km, vmt, bias, tq, tk, nsub):
    B, n, _ = qb.shape
    pairs = MLA_HEADS // 2
    tb = tq * nsub
    kern = functools.partial(_mla_kernel, tq=tq, tk=tk, n_kv=n // tk, nsub=nsub)
    return pl.pallas_call(
        kern,
        grid=(B, pairs, n // tb),
        in_specs=[pl.BlockSpec((1, tb, 2 * LANES), lambda b, j, i: (b, i, j)),
                  pl.BlockSpec((1, n, 2 * LANES), lambda b, j, i: (b, 0, j)),
                  pl.BlockSpec((1, 2 * MLA_VROWS, n), lambda b, j, i: (b, j, 0)),
                  pl.BlockSpec((META_PAD, 2 * LANES), lambda b, j, i: (0, j)),
                  pl.BlockSpec((2 * MLA_VROWS, META_PAD), lambda b, j, i: (j, 0)),
                  _full(bias.shape)],
        out_specs=pl.BlockSpec((1, tb, LANES), lambda b, j, i: (b, i, j)),
        out_shape=jax.ShapeDtypeStruct((B, n, MLA_WIDTH), F32),
        scratch_shapes=[pltpu.VMEM((MLA_VROWS, tq), F32), pltpu.VMEM((tk, tq), F32),
                        pltpu.VMEM((tk, tq), F32)],
        compiler_params=_cparams(("parallel", "parallel", "arbitrary")),
        name="mla_attn",
    )(qb, kb, vbt, km, vmt, bias)


def _post_kernel(oa_ref, ob_ref, x_ref, ga_ref, gb_ref, wout_ref, gffn_ref,
                 wrh_ref, wrl_ref, br_ref, h_ref, u_ref, route_ref):
    o = jnp.concatenate([_rms(oa_ref[...], ga_ref[...]), _rms(ob_ref[...], gb_ref[...])],
                        axis=1).astype(BF16)
    h = x_ref[...] + jnp.dot(o, wout_ref[...], preferred_element_type=F32)
    h_ref[...] = h
    u = _rms(h, gffn_ref[...])
    u_ref[...] = u

    u_hi = u.astype(BF16)
    u_lo = (u - u_hi.astype(F32)).astype(BF16)
    wrh = wrh_ref[...]
    logits = (jnp.dot(u_hi, wrh, preferred_element_type=F32)
              + jnp.dot(u_lo, wrh, preferred_element_type=F32)
              + jnp.dot(u_hi, wrl_ref[...], preferred_element_type=F32)) + br_ref[...]

    lane = lax.broadcasted_iota(jnp.int32, logits.shape, 1)
    far = jnp.int32(LANES)
    neg = jnp.float32(-jnp.inf)
    lg = jnp.where(lane < N_GROUPS, logits, neg)
    mg = jnp.max(lg, axis=1, keepdims=True)
    p_grp = 1.0 / jnp.sum(jnp.exp(lg - mg), axis=1, keepdims=True)
    grp = jnp.min(jnp.where(lg == mg, lane, far), axis=1, keepdims=True)
    e_lo = N_GROUPS + grp * EXPERTS_PER_GROUP
    le = jnp.where((lane >= e_lo) & (lane < e_lo + EXPERTS_PER_GROUP), logits, neg)
    m1 = jnp.max(le, axis=1, keepdims=True)
    i1 = jnp.min(jnp.where(le == m1, lane, far), axis=1, keepdims=True)
    le2 = jnp.where(lane == i1, neg, le)
    m2 = jnp.max(le2, axis=1, keepdims=True)
    i2 = jnp.min(jnp.where(le2 == m2, lane, far), axis=1, keepdims=True)
    t = jnp.exp(m2 - m1)
    g1 = p_grp / (1.0 + t)
    g2 = g1 * t
    e1 = (i1 - N_GROUPS).astype(F32)
    e2 = (i2 - N_GROUPS).astype(F32)
    route_ref[...] = jnp.where(lane == 0, e1, jnp.where(lane == 1, e2, jnp.where(
        lane == 2, g1, jnp.where(lane == 3, g2, 0.0))))


def _post(oa, ob, x2d, pw, tm):
    T = x2d.shape[0]
    row = lambda w: pl.BlockSpec((tm, w), lambda i: (i, 0))
    weights = (pw["ga"], pw["gb"], pw["wout"], pw["gffn"], pw["wrh"], pw["wrl"], pw["br"])
    return pl.pallas_call(
        _post_kernel,
        grid=(T // tm,),
        in_specs=[row(GQA_WIDTH), row(MLA_WIDTH), row(D_MODEL)]
        + [_full(w.shape) for w in weights],
        out_specs=[row(D_MODEL), row(D_MODEL), row(LANES)],
        out_shape=[jax.ShapeDtypeStruct((T, D_MODEL), F32),
                   jax.ShapeDtypeStruct((T, D_MODEL), F32),
                   jax.ShapeDtypeStruct((T, LANES), F32)],
        compiler_params=_cparams(("parallel",)),
        name="post",
    )(oa, ob, x2d, *weights)


def _moe_kernel(be_ref, nused_ref, tok_ref, tokn_ref, dstp_ref, dst_ref, x_hbm, wgu_ref, wd_ref,
                y_hbm, xbuf, ybuf, gsem, ssem, *, blk):
    i = pl.program_id(0)
    nb = pl.num_programs(0)
    slot = i % 2
    n_used = nused_ref[0]

    def gather_copy(t, r, s):
        return pltpu.make_async_copy(x_hbm.at[pl.ds(t, 1), :], xbuf.at[s, pl.ds(r, 1), :],
                                     gsem.at[s])

    def scatter_copy(d, r, s):
        return pltpu.make_async_copy(ybuf.at[s, pl.ds(r, 1), :], y_hbm.at[pl.ds(d, 1), :],
                                     ssem.at[s])

    def start_rows(make, idx_ref, s, unrolled):
        if unrolled:
            for r in range(blk):
                make(idx_ref[0, 0, r], r, s).start(priority=r % 2)
        else:
            def body(r, c):
                make(idx_ref[0, 0, r], r, s).start()
                return c
            lax.fori_loop(0, blk, body, 0)

    def wait_rows(make, s):
        def body(r, c):
            make(0, r, s).wait()
            return c
        lax.fori_loop(0, blk, body, 0, unroll=8)

    def experts(s):
        xe = xbuf[s].astype(BF16)
        gu = jnp.dot(xe, wgu_ref[0], preferred_element_type=F32)
        g, up = gu[:, :D_EXPERT], gu[:, D_EXPERT:]
        hid = (g * jax.nn.sigmoid(g) * up).astype(BF16)
        ybuf[s] = jnp.dot(hid, wd_ref[0], preferred_element_type=F32)

    @pl.when(i == 0)
    def _():
        start_rows(gather_copy, tok_ref, 0, False)

    wait_rows(gather_copy, slot)

    @pl.when(i >= 2)
    def _():
        wait_rows(scatter_copy, slot)

    @pl.when(i == 0)
    def _():
        start_rows(gather_copy, tokn_ref, 1 - slot, False)
        experts(slot)

    for par in range(2):
        @pl.when((i >= 1) & (i < n_used) & (slot == par))
        def _():
            start_rows(gather_copy, tokn_ref, 1 - par, True)
            experts(par)
            start_rows(scatter_copy, dstp_ref, 1 - par, True)

    @pl.when((i >= 1) & (i >= n_used))
    def _():
        start_rows(gather_copy, tokn_ref, 1 - slot, False)
        ybuf[slot] = jnp.zeros((blk, D_MODEL), F32)
        start_rows(scatter_copy, dstp_ref, 1 - slot, False)

    @pl.when(i == nb - 1)
    def _():
        start_rows(scatter_copy, dst_ref, slot, False)
        wait_rows(gather_copy, 1 - slot)

        @pl.when(i >= 1)
        def _():
            wait_rows(scatter_copy, 1 - slot)
        wait_rows(scatter_copy, slot)


def _moe_plan(eid, blk):
    T = eid.shape[0]
    A = 2 * T
    n_blocks = (A + N_EXPERTS * (blk - 1) + blk - 1) // blk
    P = n_blocks * blk
    e_flat = eid.T.reshape(A)
    onehot = (e_flat[:, None] == jnp.arange(N_EXPERTS, dtype=jnp.int32)[None, :]).astype(jnp.int32)
    chunks = A // PLAN_CHUNK
    tri = jnp.tril(jnp.ones((PLAN_CHUNK, PLAN_CHUNK), BF16))
    within = jnp.einsum("ij,cjk->cik", tri,
                        onehot.astype(BF16).reshape(chunks, PLAN_CHUNK, N_EXPERTS),
                        preferred_element_type=F32).astype(jnp.int32)
    totals = within[:, -1, :]
    csum = (within + (jnp.cumsum(totals, axis=0) - totals)[:, None, :]).reshape(A, N_EXPERTS)
    rank = jnp.sum(csum * onehot, axis=1) - 1
    counts = csum[-1]
    padded = (counts + blk - 1) // blk * blk
    pends = jnp.cumsum(padded)
    pstarts = pends - padded
    slot = jnp.sum(onehot * pstarts[None, :], axis=1) + rank
    a_idx = jnp.arange(A, dtype=jnp.int32)
    buf_a = jnp.full((P,), -1, jnp.int32).at[slot].set(a_idx)
    invalid = buf_a < 0
    buf_tok = jnp.where(invalid, 0, buf_a % T)
    buf_dst = jnp.where(invalid, A + jnp.cumsum(invalid.astype(jnp.int32)) - 1, buf_a)
    block_expert = jnp.minimum(
        jnp.searchsorted(pends, jnp.arange(n_blocks, dtype=jnp.int32) * blk, side="right"),
        N_EXPERTS - 1).astype(jnp.int32)
    n_used = (pends[-1] // blk).astype(jnp.int32).reshape(1)
    return block_expert, n_used, buf_tok, buf_dst, n_blocks


def _moe(u2d, eid, pw, blk):
    block_expert, n_used, buf_tok, buf_dst, n_blocks = _moe_plan(eid, blk)
    P = n_blocks * blk
    tok3 = buf_tok.reshape(n_blocks, 1, blk)
    dst3 = buf_dst.reshape(n_blocks, 1, blk)
    last = n_blocks - 1
    idx_spec = lambda f: pl.BlockSpec((1, 1, blk), f, memory_space=pltpu.SMEM)
    grid_spec = pltpu.PrefetchScalarGridSpec(
        num_scalar_prefetch=2,
        grid=(n_blocks,),
        in_specs=[idx_spec(lambda i, be, nu: (i, 0, 0)),
                  idx_spec(lambda i, be, nu: (jnp.minimum(i + 1, last), 0, 0)),
                  idx_spec(lambda i, be, nu: (jnp.maximum(i - 1, 0), 0, 0)),
                  idx_spec(lambda i, be, nu: (i, 0, 0)),
                  pl.BlockSpec(memory_space=pl.ANY),
                  pl.BlockSpec((1, D_MODEL, 2 * D_EXPERT), lambda i, be, nu: (be[i], 0, 0)),
                  pl.BlockSpec((1, D_EXPERT, D_MODEL), lambda i, be, nu: (be[i], 0, 0))],
        out_specs=pl.BlockSpec(memory_space=pl.ANY),
        scratch_shapes=[pltpu.VMEM((2, blk, D_MODEL), F32), pltpu.VMEM((2, blk, D_MODEL), F32),
                        pltpu.SemaphoreType.DMA((2,)), pltpu.SemaphoreType.DMA((2,))],
    )
    return pl.pallas_call(
        functools.partial(_moe_kernel, blk=blk),
        grid_spec=grid_spec,
        out_shape=jax.ShapeDtypeStruct((P, D_MODEL), F32),
        compiler_params=_cparams(("arbitrary",)),
        name="moe",
    )(block_expert, n_used, tok3, tok3, dst3, dst3, u2d, pw["wgu"], pw["wd"])


def _final_kernel(h_ref, y1_ref, y2_ref, route_ref, g_ref, o_ref):
    r = route_ref[...]
    y = y1_ref[...] * r[:, 2:3] + y2_ref[...] * r[:, 3:4]
    o_ref[...] = _rms(h_ref[...] + y, g_ref[...])


def _final(h2d, y, route, gfin, tm):
    T = h2d.shape[0]
    second = T // tm
    return pl.pallas_call(
        _final_kernel,
        grid=(T // tm,),
        in_specs=[pl.BlockSpec((tm, D_MODEL), lambda i: (i, 0)),
                  pl.BlockSpec((tm, D_MODEL), lambda i: (i, 0)),
                  pl.BlockSpec((tm, D_MODEL), lambda i: (second + i, 0)),
                  pl.BlockSpec((tm, LANES), lambda i: (i, 0)),
                  _full(gfin.shape)],
        out_specs=pl.BlockSpec((tm, D_MODEL), lambda i: (i, 0)),
        out_shape=jax.ShapeDtypeStruct((T, D_MODEL), F32),
        compiler_params=_cparams(("parallel",)),
        name="final",
    )(h2d, y, y, route, gfin)


def _pair_swap(w):
    s = w.shape
    return w.reshape(s[:-1] + (s[-1] // 2, 2))[..., ::-1].reshape(s)


def _prep_weights(norm_mix_g, w_in, gqa_q_norm_g, gqa_k_norm_g, mla_q_norm_g, mla_kv_norm_g,
                  w_mla_q_up, w_mla_kv_up, gqa_out_norm_g, mla_out_norm_g, w_out, norm_ffn_g,
                  w_router_group, b_router_group, w_router_expert, b_router_expert,
                  w_expert_gate, w_expert_up, w_expert_down, norm_final_g):
    D = D_MODEL
    order = jnp.array(GQA_HEAD_ORDER)
    wq = w_in[:, :512].reshape(D, GQA_HEADS, GQA_HEAD_DIM)[:, order].reshape(D, GQA_WIDTH)
    wk = w_in[:, 512:640]
    wkr = w_in[:, 1152:1184]
    z64 = jnp.zeros((D, 64), F32)
    z32 = jnp.zeros((D, 32), F32)
    w1 = jnp.concatenate([
        wq, _pair_swap(wq), wk, _pair_swap(wk), w_in[:, 640:768], w_in[:, 768:1024],
        w_in[:, 1024:1152], jnp.concatenate([z64, wkr, z32], 1),
        jnp.concatenate([z64, _pair_swap(wkr), z32], 1)], axis=1).astype(BF16)

    qu = w_mla_q_up.reshape(MLA_Q_RANK, MLA_HEADS, MLA_QK_DIM)
    zq32 = jnp.zeros((MLA_Q_RANK, MLA_HEADS, 32), F32)
    zq64 = jnp.zeros((MLA_Q_RANK, MLA_HEADS, 64), F32)
    wqu = jnp.concatenate([qu, zq32], axis=2).reshape(MLA_Q_RANK, MLA_HEADS * LANES)
    wqus = jnp.concatenate([zq64, _pair_swap(qu[:, :, MLA_NOPE_DIM:]), zq32],
                           axis=2).reshape(MLA_Q_RANK, MLA_HEADS * LANES)
    kvu = w_mla_kv_up.reshape(MLA_KV_RANK, MLA_HEADS, MLA_NOPE_DIM + MLA_V_DIM)
    wku = jnp.concatenate([kvu[:, :, :MLA_NOPE_DIM],
                           jnp.zeros((MLA_KV_RANK, MLA_HEADS, 64), F32)],
                          axis=2).reshape(MLA_KV_RANK, MLA_HEADS * LANES)
    wvu = kvu[:, :, MLA_NOPE_DIM:].reshape(MLA_KV_RANK, MLA_WIDTH)

    head_block = jnp.full((GQA_HEAD_DIM, GQA_HEAD_DIM), 1.0 / GQA_HEAD_DIM, F32)
    perm_rows = lambda a: a.reshape((GQA_HEADS, GQA_HEAD_DIM) + a.shape[1:])[order].reshape(a.shape)
    wr = jnp.concatenate([w_router_group, w_router_expert,
                          jnp.zeros((D, LANES - N_GROUPS - N_EXPERTS), F32)], axis=1)
    wrh = wr.astype(BF16)
    br = jnp.concatenate([b_router_group, b_router_expert,
                          jnp.zeros((LANES - N_GROUPS - N_EXPERTS,), F32)])[None, :]
    return dict(
        gmix=norm_mix_g[None, :], w1=w1,
        gq=jnp.tile(gqa_q_norm_g, GQA_HEADS)[None, :],
        gqs=jnp.tile(_pair_swap(gqa_q_norm_g), GQA_HEADS)[None, :],
        gk=jnp.tile(gqa_k_norm_g, GQA_KV_HEADS)[None, :],
        gks=jnp.tile(_pair_swap(gqa_k_norm_g), GQA_KV_HEADS)[None, :],
        gcq=mla_q_norm_g[None, :], gckv=mla_kv_norm_g[None, :],
        wqu=wqu.astype(BF16), wqus=wqus.astype(BF16), wku=wku.astype(BF16),
        wvu=wvu.astype(BF16),
        bdq=jnp.kron(jnp.eye(GQA_HEADS, dtype=F32), head_block).astype(BF16),
        bdk=jnp.kron(jnp.eye(GQA_KV_HEADS, dtype=F32), head_block).astype(BF16),
        ga=perm_rows(gqa_out_norm_g)[None, :], gb=mla_out_norm_g[None, :],
        wout=jnp.concatenate([perm_rows(w_out[:GQA_WIDTH]), w_out[GQA_WIDTH:]], 0).astype(BF16),
        gffn=norm_ffn_g[None, :], wrh=wrh, wrl=(wr - wrh.astype(F32)).astype(BF16), br=br,
        wgu=jnp.concatenate([w_expert_gate, w_expert_up], axis=2).astype(BF16),
        wd=w_expert_down.astype(BF16), gfin=norm_final_g[None, :])


def _angles(rows, cols, rot_dim):
    half = rot_dim // 2
    inv = ROPE_THETA ** (-jnp.arange(0, half, 2, dtype=F32) / half)
    ang = jnp.concatenate([rows.astype(F32)[:, None] * inv,
                           cols.astype(F32)[:, None] * inv], axis=-1)
    return jnp.cos(ang), jnp.sin(ang)


def _rope_tables(rows, cols):
    L = rows.shape[0]
    lanes = lambda a: jnp.repeat(a, 2, axis=-1)
    sign = lambda w: jnp.tile(jnp.array([-1.0, 1.0], F32), w // 2)[None, :]
    cg, sg = _angles(rows, cols, GQA_HEAD_DIM)
    cg = jnp.tile(lanes(cg), (1, 2))
    sg = jnp.tile(lanes(sg) * sign(GQA_HEAD_DIM), (1, 2))
    cm, sm = _angles(rows, cols, MLA_ROPE_DIM)
    cm, sm = lanes(cm), lanes(sm) * sign(MLA_ROPE_DIM)
    one64, z64, z32 = jnp.ones((L, 64), F32), jnp.zeros((L, 64), F32), jnp.zeros((L, 32), F32)
    cmq = jnp.concatenate([one64, cm, z32], 1) * SCALE_B
    smq = jnp.concatenate([z64, sm, z32], 1) * SCALE_B
    cmk = jnp.concatenate([z64, cm, z32], 1)
    smk = jnp.concatenate([z64, sm, z32], 1)
    return (cg, sg, cmq, smq, cmk, smk)


def _tile_sizes(n):
    tm = min(256, n)
    tq_gqa = min(128, n)
    tq_mla = min(1024, n)
    tk = min(512, n // KV_UNROLL)
    return tm, (tq_gqa, min(4, n // tq_gqa)), (tq_mla, min(2, n // tq_mla)), tk


MOE_ROWS = 256
PLAN_CHUNK = 256


def _with_ones(vt, groups):
    lead, (rows, n) = vt.shape[:-2], vt.shape[-2:]
    g = vt.reshape(lead + (groups, rows // groups, n))
    ones = jnp.ones(lead + (groups, ONES_ROWS, n), vt.dtype)
    return jnp.concatenate([g, ones], axis=-2).reshape(lead + (rows + groups * ONES_ROWS, n))


def _trunk(x, pw, meta_kv, bias):
    B, n, D = x.shape
    T = B * n
    tm, gqa_tiles, mla_tiles, tk = _tile_sizes(n)
    t = jnp.arange(n, dtype=jnp.int32)
    tables = _rope_tables(t // GRID_W, t % GRID_W)
    x2d = x.reshape(T, D)
    qa, ka, va, qb, kb, vb = _project(x2d, tables, pw, tm)
    ka_m, vat_m, kb_m, vbt_m = meta_kv
    r3 = lambda a: a.reshape(B, n, a.shape[-1])
    t3 = lambda a: jnp.swapaxes(r3(a), 1, 2)
    oa = _gqa_attention(r3(qa), r3(ka), _with_ones(t3(va), 1), ka_m, vat_m, bias,
                        gqa_tiles[0], tk, gqa_tiles[1])
    ob = _mla_attention(r3(qb), r3(kb), _with_ones(t3(vb), MLA_HEADS), kb_m, vbt_m, bias,
                        mla_tiles[0], tk, mla_tiles[1])
    h, u, route = _post(oa.reshape(T, GQA_WIDTH), ob.reshape(T, MLA_WIDTH), x2d, pw, tm)
    eid = route[:, :2].astype(jnp.int32)
    y = _moe(u, eid, pw, MOE_ROWS)
    out = _final(h, y, route, pw["gfin"], tm)
    return out.reshape(B, n, D)


def kernel(x_prompt, x_sample, meta_tokens, norm_mix_g, w_in, gqa_q_norm_g, gqa_k_norm_g, mla_q_norm_g, mla_kv_norm_g, w_mla_q_up, w_mla_kv_up, gqa_out_norm_g, mla_out_norm_g, w_out, norm_ffn_g, w_router_group, b_router_group, w_router_expert, b_router_expert, w_expert_gate, w_expert_up, w_expert_down, norm_final_g):
    pw = _prep_weights(norm_mix_g[0], w_in[0], gqa_q_norm_g[0], gqa_k_norm_g[0],
                       mla_q_norm_g[0], mla_kv_norm_g[0], w_mla_q_up[0], w_mla_kv_up[0],
                       gqa_out_norm_g[0], mla_out_norm_g[0], w_out[0], norm_ffn_g[0],
                       w_router_group[0], b_router_group[0], w_router_expert[0],
                       b_router_expert[0], w_expert_gate[0], w_expert_up[0],
                       w_expert_down[0], norm_final_g)
    meta_x = jnp.concatenate([meta_tokens.astype(F32),
                              jnp.zeros((META_PAD - N_META, D_MODEL), F32)], axis=0)
    mpos = jnp.arange(META_PAD, dtype=jnp.int32)
    meta_tables = _rope_tables(jnp.full((META_PAD,), -1, jnp.int32),
                               jnp.where(mpos < N_META, mpos, 0))
    _, ka_m, va_m, _, kb_m, vb_m = _project(meta_x, meta_tables, pw, META_PAD)
    bias = jnp.where(mpos < N_META, 0.0, NEG_BIG).astype(F32)[:, None]
    meta_kv = (ka_m, _with_ones(va_m.T, 1), kb_m, _with_ones(vb_m.T, MLA_HEADS))
    return (_trunk(x_prompt, pw, meta_kv, bias), _trunk(x_sample, pw, meta_kv, bias))
```

```python
import functools
import math

import jax
import jax.numpy as jnp
from jax import lax
from jax.experimental import pallas as pl
from jax.experimental.pallas import tpu as pltpu

F32 = jnp.float32
BF16 = jnp.bfloat16

D_MODEL = 1024
N_META = 16
GRID_W = 64
ROPE_THETA = 10000.0
NORM_EPS = 1e-6

GQA_HEADS = 8
GQA_KV_HEADS = 2
GQA_HEAD_DIM = 64
GQA_WIDTH = GQA_HEADS * GQA_HEAD_DIM

MLA_HEADS = 8
MLA_Q_RANK = 256
MLA_KV_RANK = 128
MLA_NOPE_DIM = 64
MLA_ROPE_DIM = 32
MLA_V_DIM = 64
MLA_QK_DIM = MLA_NOPE_DIM + MLA_ROPE_DIM
MLA_WIDTH = MLA_HEADS * MLA_V_DIM

N_GROUPS = 4
EXPERTS_PER_GROUP = 8
N_EXPERTS = N_GROUPS * EXPERTS_PER_GROUP
D_EXPERT = 256

LANES = 128
HALF = LANES // 2
META_PAD = 128
ONES_ROWS = 16
MLA_VROWS = MLA_V_DIM + ONES_ROWS
KV_UNROLL = 8
NEG_BIG = -1e30
VMEM_LIMIT = 56 * 1024 * 1024

LOG2E = 1.4426950408889634
SCALE_A = GQA_HEAD_DIM ** -0.5 * LOG2E
SCALE_B = MLA_QK_DIM ** -0.5 * LOG2E

GQA_HEAD_ORDER = (0, 4, 1, 5, 2, 6, 3, 7)

_C_Q, _C_QS, _C_K, _C_KS, _C_V, _C_CQ, _C_CKV, _C_KR, _C_KRS, _C_END = (
    0, 512, 1024, 1152, 1280, 1408, 1664, 1792, 1920, 2048)

NT_DIMS = (((1,), (1,)), ((), ()))


def _cparams(sem):
    return pltpu.CompilerParams(dimension_semantics=sem, vmem_limit_bytes=VMEM_LIMIT)


def _full(shape):
    nd = len(shape)
    return pl.BlockSpec(shape, lambda *_: (0,) * nd)


def _rms(x, g):
    return x * lax.rsqrt(jnp.mean(x * x, axis=-1, keepdims=True) + NORM_EPS) * g


def _tile_lanes(a, reps):
    return jnp.concatenate([a] * reps, axis=1)


def _head_mean_sq(z, bd):
    sq = z * z
    hi = sq.astype(BF16)
    lo = (sq - hi.astype(F32)).astype(BF16)
    return (jnp.dot(hi, bd, preferred_element_type=F32)
            + jnp.dot(lo, bd, preferred_element_type=F32))


def _proj_kernel(x_ref, cg_ref, sg_ref, cmq_ref, smq_ref, cmk_ref, smk_ref,
                 gmix_ref, w1_ref, gq_ref, gqs_ref, gk_ref, gks_ref, gcq_ref, gckv_ref,
                 wqu_ref, wqus_ref, wku_ref, wvu_ref, bdq_ref, bdk_ref,
                 qa_ref, ka_ref, va_ref, qb_ref, kb_ref, vb_ref):
    u = _rms(x_ref[...], gmix_ref[...]).astype(BF16)
    z = jnp.dot(u, w1_ref[...], preferred_element_type=F32)

    cg, sg = cg_ref[...], sg_ref[...]
    zq, zqs = z[:, _C_Q:_C_QS], z[:, _C_QS:_C_K]
    rq = lax.rsqrt(_head_mean_sq(zq, bdq_ref[...]) + NORM_EPS) * SCALE_A
    c4, s4 = _tile_lanes(cg, 4), _tile_lanes(sg, 4)
    qa_ref[...] = (rq * (zq * gq_ref[...] * c4 + zqs * gqs_ref[...] * s4)).astype(BF16)
    zk, zks = z[:, _C_K:_C_KS], z[:, _C_KS:_C_V]
    rk = lax.rsqrt(_head_mean_sq(zk, bdk_ref[...]) + NORM_EPS)
    ka_ref[...] = (rk * (zk * gk_ref[...] * cg + zks * gks_ref[...] * sg)).astype(BF16)
    va_ref[...] = z[:, _C_V:_C_CQ].astype(BF16)

    cqn = _rms(z[:, _C_CQ:_C_CKV], gcq_ref[...]).astype(BF16)
    zqb = jnp.dot(cqn, wqu_ref[...], preferred_element_type=F32)
    zqbs = jnp.dot(cqn, wqus_ref[...], preferred_element_type=F32)
    qb_ref[...] = (zqb * _tile_lanes(cmq_ref[...], MLA_HEADS)
                   + zqbs * _tile_lanes(smq_ref[...], MLA_HEADS)).astype(BF16)
    ckvn = _rms(z[:, _C_CKV:_C_KR], gckv_ref[...]).astype(BF16)
    kn = jnp.dot(ckvn, wku_ref[...], preferred_element_type=F32)
    kr = z[:, _C_KR:_C_KRS] * cmk_ref[...] + z[:, _C_KRS:_C_END] * smk_ref[...]
    kb_ref[...] = (kn + _tile_lanes(kr, MLA_HEADS)).astype(BF16)
    vb_ref[...] = jnp.dot(ckvn, wvu_ref[...], preferred_element_type=F32).astype(BF16)


def _project(x2d, tables, pw, tm):
    T = x2d.shape[0]
    n_pos = tables[0].shape[0]
    pos_blocks = n_pos // tm
    row = lambda w: pl.BlockSpec((tm, w), lambda i: (i, 0))
    tab = pl.BlockSpec((tm, LANES), lambda i: (i % pos_blocks, 0))
    weights = (pw["gmix"], pw["w1"], pw["gq"], pw["gqs"], pw["gk"], pw["gks"], pw["gcq"],
               pw["gckv"], pw["wqu"], pw["wqus"], pw["wku"], pw["wvu"], pw["bdq"], pw["bdk"])
    widths = (GQA_WIDTH, LANES, LANES, MLA_HEADS * LANES, MLA_HEADS * LANES, MLA_WIDTH)
    return pl.pallas_call(
        _proj_kernel,
        grid=(T // tm,),
        in_specs=[row(D_MODEL)] + [tab] * 6 + [_full(w.shape) for w in weights],
        out_specs=[row(w) for w in widths],
        out_shape=[jax.ShapeDtypeStruct((T, w), BF16) for w in widths],
        compiler_params=_cparams(("parallel",)),
        name="proj",
    )(x2d, *tables, *weights)


def _flash(load_q, load_k, load_vt, km, vmt, bias, n_kv, tk, vdim, acc_ref, s_refs):
    assert n_kv >= KV_UNROLL and n_kv % KV_UNROLL == 0 and KV_UNROLL % 2 == 0

    def scores(j):
        off = pl.multiple_of(j * tk, tk)
        return lax.dot_general(load_k(off), load_q(), NT_DIMS, preferred_element_type=F32)

    def consume(s, j, m_prev):
        off = pl.multiple_of(j * tk, tk)
        m_new = jnp.maximum(m_prev, jnp.max(s, axis=0, keepdims=True))
        alpha = jnp.exp2(m_prev - m_new)
        p = jnp.exp2(s - m_new)
        acc_ref[...] = alpha * acc_ref[...] + jnp.dot(
            load_vt(off), p.astype(BF16), preferred_element_type=F32)
        return m_new

    s = lax.dot_general(km, load_q(), NT_DIMS, preferred_element_type=F32) + bias
    m0 = jnp.max(s, axis=0, keepdims=True)
    acc_ref[...] = jnp.dot(vmt, jnp.exp2(s - m0).astype(BF16), preferred_element_type=F32)
    s_refs[0][...] = scores(0)

    def group(i, m, last):
        j0 = KV_UNROLL * i
        for u in range(KV_UNROLL):
            if not (last and u == KV_UNROLL - 1):
                s_refs[(u + 1) % 2][...] = scores(j0 + u + 1)
            m = consume(s_refs[u % 2][...], j0 + u, m)
        return m

    trips = n_kv // KV_UNROLL
    m = lax.fori_loop(0, trips - 1, lambda i, c: group(i, c, False), m0)
    group(trips - 1, m, True)
    acc = acc_ref[...]
    return acc[:vdim] * (1.0 / acc[vdim:vdim + 1])


def _gqa_kernel(q_ref, k_ref, vt_ref, km_ref, vmt_ref, bias_ref, o_ref,
                qs_ref, acc_ref, s0_ref, s1_ref, *, tq, tk, n_kv, nsub):
    lo = lax.broadcasted_iota(jnp.int32, (tq, LANES), 1) < HALF
    zero = jnp.zeros((tq, LANES), BF16)

    def sub_tile(i, carry):
        rows = pl.ds(pl.multiple_of(i * tq, tq), tq)
        for c in range(GQA_HEADS // 2):
            qc = q_ref[0, rows, c * LANES:(c + 1) * LANES]
            qs_ref[(2 * c) * tq:(2 * c + 1) * tq, :] = jnp.where(lo, qc, zero)
            qs_ref[(2 * c + 1) * tq:(2 * c + 2) * tq, :] = jnp.where(lo, zero, qc)
        ot = _flash(lambda: qs_ref[...],
                    lambda off: k_ref[0, pl.ds(off, tk), :],
                    lambda off: vt_ref[0, :, pl.ds(off, tk)],
                    km_ref[...], vmt_ref[...], bias_ref[...], n_kv, tk, LANES,
                    acc_ref, (s0_ref, s1_ref))
        for c in range(GQA_HEADS // 2):
            pair = jnp.concatenate([ot[:HALF, (2 * c) * tq:(2 * c + 1) * tq],
                                    ot[HALF:, (2 * c + 1) * tq:(2 * c + 2) * tq]], axis=0)
            o_ref[0, rows, c * LANES:(c + 1) * LANES] = pair.T
        return carry

    lax.fori_loop(0, nsub, sub_tile, 0)


def _gqa_attention(qa, ka, vat, km, vmt, bias, tq, tk, nsub):
    B, n, _ = qa.shape
    R = GQA_HEADS * tq
    tb = tq * nsub
    kern = functools.partial(_gqa_kernel, tq=tq, tk=tk, n_kv=n // tk, nsub=nsub)
    return pl.pallas_call(
        kern,
        grid=(B, n // tb),
        in_specs=[pl.BlockSpec((1, tb, GQA_WIDTH), lambda b, i: (b, i, 0)),
                  pl.BlockSpec((1, n, LANES), lambda b, i: (b, 0, 0)),
                  pl.BlockSpec((1, LANES + ONES_ROWS, n), lambda b, i: (b, 0, 0)),
                  _full(km.shape), _full(vmt.shape), _full(bias.shape)],
        out_specs=pl.BlockSpec((1, tb, GQA_WIDTH), lambda b, i: (b, i, 0)),
        out_shape=jax.ShapeDtypeStruct((B, n, GQA_WIDTH), F32),
        scratch_shapes=[pltpu.VMEM((R, LANES), BF16), pltpu.VMEM((LANES + ONES_ROWS, R), F32),
                        pltpu.VMEM((tk, R), F32), pltpu.VMEM((tk, R), F32)],
        compiler_params=_cparams(("parallel", "arbitrary")),
        name="gqa_attn",
    )(qa, ka, vat, km, vmt, bias)


def _mla_kernel(q_ref, k_ref, vt_ref, km_ref, vmt_ref, bias_ref, o_ref, acc_ref,
                s0_ref, s1_ref, *, tq, tk, n_kv, nsub):
    def sub_tile(i, carry):
        rows = pl.ds(pl.multiple_of(i * tq, tq), tq)
        outs = []
        for h in range(2):
            sl = slice(h * LANES, (h + 1) * LANES)
            vs = slice(h * MLA_VROWS, (h + 1) * MLA_VROWS)
            outs.append(_flash(lambda: q_ref[0, rows, sl],
                               lambda off: k_ref[0, pl.ds(off, tk), sl],
                               lambda off: vt_ref[0, vs, pl.ds(off, tk)],
                               km_ref[:, sl], vmt_ref[vs, :], bias_ref[...], n_kv, tk,
                               MLA_V_DIM, acc_ref, (s0_ref, s1_ref)))
        o_ref[0, rows, :] = jnp.concatenate(outs, axis=0).T
        return carry

    lax.fori_loop(0, nsub, sub_tile, 0)


def _mla_attention(qb, kb, vbt, km, vmt, bias, tq, tk, nsub):
    B, n, _ = qb.shape
    pairs = MLA_HEADS // 2
    tb = tq * nsub
    kern = functools.partial(_mla_kernel, tq=tq, tk=tk, n_kv=n // tk, nsub=nsub)
    return pl.pallas_call(
        kern,
        grid=(B, pairs, n // tb),
        in_specs=[pl.BlockSpec((1, tb, 2 * LANES), lambda b, j, i: (b, i, j)),
                  pl.BlockSpec((1, n, 2 * LANES), lambda b, j, i: (b, 0, j)),
                  pl.BlockSpec((1, 2 * MLA_VROWS, n), lambda b, j, i: (b, j, 0)),
                  pl.BlockSpec((META_PAD, 2 * LANES), lambda b, j, i: (0, j)),
                  pl.BlockSpec((2 * MLA_VROWS, META_PAD), lambda b, j, i: (j, 0)),
                  _full(bias.shape)],
        out_specs=pl.BlockSpec((1, tb, LANES), lambda b, j, i: (b, i, j)),
        out_shape=jax.ShapeDtypeStruct((B, n, MLA_WIDTH), F32),
        scratch_shapes=[pltpu.VMEM((MLA_VROWS, tq), F32), pltpu.VMEM((tk, tq), F32),
                        pltpu.VMEM((tk, tq), F32)],
        compiler_params=_cparams(("parallel", "parallel", "arbitrary")),
        name="mla_attn",
    )(qb, kb, vbt, km, vmt, bias)


def _post_kernel(oa_ref, ob_ref, x_ref, ga_ref, gb_ref, wout_ref, gffn_ref,
                 wrh_ref, wrl_ref, br_ref, h_ref, u_ref, route_ref):
    o = jnp.concatenate([_rms(oa_ref[...], ga_ref[...]), _rms(ob_ref[...], gb_ref[...])],
                        axis=1).astype(BF16)
    h = x_ref[...] + jnp.dot(o, wout_ref[...], preferred_element_type=F32)
    h_ref[...] = h
    u = _rms(h, gffn_ref[...])
    u_ref[...] = u

    u_hi = u.astype(BF16)
    u_lo = (u - u_hi.astype(F32)).astype(BF16)
    wrh = wrh_ref[...]
    logits = (jnp.dot(u_hi, wrh, preferred_element_type=F32)
              + jnp.dot(u_lo, wrh, preferred_element_type=F32)
              + jnp.dot(u_hi, wrl_ref[...], preferred_element_type=F32)) + br_ref[...]

    lane = lax.broadcasted_iota(jnp.int32, logits.shape, 1)
    far = jnp.int32(LANES)
    neg = jnp.float32(-jnp.inf)
    lg = jnp.where(lane < N_GROUPS, logits, neg)
    mg = jnp.max(lg, axis=1, keepdims=True)
    p_grp = 1.0 / jnp.sum(jnp.exp(lg - mg), axis=1, keepdims=True)
    grp = jnp.min(jnp.where(lg == mg, lane, far), axis=1, keepdims=True)
    e_lo = N_GROUPS + grp * EXPERTS_PER_GROUP
    le = jnp.where((lane >= e_lo) & (lane < e_lo + EXPERTS_PER_GROUP), logits, neg)
    m1 = jnp.max(le, axis=1, keepdims=True)
    i1 = jnp.min(jnp.where(le == m1, lane, far), axis=1, keepdims=True)
    le2 = jnp.where(lane == i1, neg, le)
    m2 = jnp.max(le2, axis=1, keepdims=True)
    i2 = jnp.min(jnp.where(le2 == m2, lane, far), axis=1, keepdims=True)
    t = jnp.exp(m2 - m1)
    g1 = p_grp / (1.0 + t)
    g2 = g1 * t
    e1 = (i1 - N_GROUPS).astype(F32)
    e2 = (i2 - N_GROUPS).astype(F32)
    route_ref[...] = jnp.where(lane == 0, e1, jnp.where(lane == 1, e2, jnp.where(
        lane == 2, g1, jnp.where(lane == 3, g2, 0.0))))


def _post(oa, ob, x2d, pw, tm):
    T = x2d.shape[0]
    row = lambda w: pl.BlockSpec((tm, w), lambda i: (i, 0))
    weights = (pw["ga"], pw["gb"], pw["wout"], pw["gffn"], pw["wrh"], pw["wrl"], pw["br"])
    return pl.pallas_call(
        _post_kernel,
        grid=(T // tm,),
        in_specs=[row(GQA_WIDTH), row(MLA_WIDTH), row(D_MODEL)]
        + [_full(w.shape) for w in weights],
        out_specs=[row(D_MODEL), row(D_MODEL), row(LANES)],
        out_shape=[jax.ShapeDtypeStruct((T, D_MODEL), F32),
                   jax.ShapeDtypeStruct((T, D_MODEL), F32),
                   jax.ShapeDtypeStruct((T, LANES), F32)],
        compiler_params=_cparams(("parallel",)),
        name="post",
    )(oa, ob, x2d, *weights)


def _moe_kernel(be_ref, nused_ref, tok_ref, tokn_ref, dstp_ref, dst_ref, x_hbm, wgu_ref, wd_ref,
                y_hbm, xbuf, ybuf, gsem, ssem, *, blk):
    i = pl.program_id(0)
    nb = pl.num_programs(0)
    slot = i % 2
    n_used = nused_ref[0]

    def gather_copy(t, r, s):
        return pltpu.make_async_copy(x_hbm.at[pl.ds(t, 1), :], xbuf.at[s, pl.ds(r, 1), :],
                                     gsem.at[s])

    def scatter_copy(d, r, s):
        return pltpu.make_async_copy(ybuf.at[s, pl.ds(r, 1), :], y_hbm.at[pl.ds(d, 1), :],
                                     ssem.at[s])

    def start_rows(make, idx_ref, s, unrolled):
        if unrolled:
            for r in range(blk):
                make(idx_ref[0, 0, r], r, s).start(priority=r % 2)
        else:
            def body(r, c):
                make(idx_ref[0, 0, r], r, s).start()
                return c
            lax.fori_loop(0, blk, body, 0)

    def wait_rows(make, s):
        def body(r, c):
            make(0, r, s).wait()
            return c
        lax.fori_loop(0, blk, body, 0, unroll=8)

    def experts(s):
        xe = xbuf[s].astype(BF16)
        gu = jnp.dot(xe, wgu_ref[0], preferred_element_type=F32)
        g, up = gu[:, :D_EXPERT], gu[:, D_EXPERT:]
        hid = (g * jax.nn.sigmoid(g) * up).astype(BF16)
        ybuf[s] = jnp.dot(hid, wd_ref[0], preferred_element_type=F32)

    @pl.when(i == 0)
    def _():
        start_rows(gather_copy, tok_ref, 0, False)

    wait_rows(gather_copy, slot)

    @pl.when(i >= 2)
    def _():
        wait_rows(scatter_copy, slot)

    @pl.when(i == 0)
    def _():
        start_rows(gather_copy, tokn_ref, 1 - slot, False)
        experts(slot)

    for par in range(2):
        @pl.when((i >= 1) & (i < n_used) & (slot == par))
        def _():
            start_rows(gather_copy, tokn_ref, 1 - par, True)
            experts(par)
            start_rows(scatter_copy, dstp_ref, 1 - par, True)

    @pl.when((i >= 1) & (i >= n_used))
    def _():
        start_rows(gather_copy, tokn_ref, 1 - slot, False)
        ybuf[slot] = jnp.zeros((blk, D_MODEL), F32)
        start_rows(scatter_copy, dstp_ref, 1 - slot, False)

    @pl.when(i == nb - 1)
    def _():
        start_rows(scatter_copy, dst_ref, slot, False)
        wait_rows(gather_copy, 1 - slot)

        @pl.when(i >= 1)
        def _():
            wait_rows(scatter_copy, 1 - slot)
        wait_rows(scatter_copy, slot)


def _moe_plan(eid, blk):
    T = eid.shape[0]
    A = 2 * T
    n_blocks = (A + N_EXPERTS * (blk - 1) + blk - 1) // blk
    P = n_blocks * blk
    e_flat = eid.T.reshape(A)
    onehot = (e_flat[:, None] == jnp.arange(N_EXPERTS, dtype=jnp.int32)[None, :]).astype(jnp.int32)
    chunks = A // PLAN_CHUNK
    tri = jnp.tril(jnp.ones((PLAN_CHUNK, PLAN_CHUNK), BF16))
    within = jnp.einsum("ij,cjk->cik", tri,
                        onehot.astype(BF16).reshape(chunks, PLAN_CHUNK, N_EXPERTS),
                        preferred_element_type=F32).astype(jnp.int32)
    totals = within[:, -1, :]
    csum = (within + (jnp.cumsum(totals, axis=0) - totals)[:, None, :]).reshape(A, N_EXPERTS)
    rank = jnp.sum(csum * onehot, axis=1) - 1
    counts = csum[-1]
    padded = (counts + blk - 1) // blk * blk
    pends = jnp.cumsum(padded)
    pstarts = pends - padded
    slot = jnp.sum(onehot * pstarts[None, :], axis=1) + rank
    a_idx = jnp.arange(A, dtype=jnp.int32)
    buf_a = jnp.full((P,), -1, jnp.int32).at[slot].set(
        a_idx, unique_indices=True, mode="promise_in_bounds")
    invalid = buf_a < 0
    buf_tok = jnp.where(invalid, 0, buf_a % T)
    block_start = jnp.arange(n_blocks, dtype=jnp.int32) * blk
    block_expert = jnp.minimum(
        jnp.sum((pends[None, :] <= block_start[:, None]).astype(jnp.int32), axis=1),
        N_EXPERTS - 1)
    pad_cnt = padded - counts
    pad_before = jnp.cumsum(pad_cnt) - pad_cnt
    block_base = (pad_before - pstarts - counts)[block_expert]
    s_idx = jnp.arange(P, dtype=jnp.int32)
    pad_row = jnp.where(s_idx < pends[-1], A + jnp.repeat(block_base, blk) + s_idx, s_idx)
    buf_dst = jnp.where(invalid, pad_row, buf_a)
    n_used = (pends[-1] // blk).astype(jnp.int32).reshape(1)
    return block_expert, n_used, buf_tok, buf_dst, n_blocks


def _moe(u2d, eid, pw, blk):
    block_expert, n_used, buf_tok, buf_dst, n_blocks = _moe_plan(eid, blk)
    P = n_blocks * blk
    tok3 = buf_tok.reshape(n_blocks, 1, blk)
    dst3 = buf_dst.reshape(n_blocks, 1, blk)
    last = n_blocks - 1
    idx_spec = lambda f: pl.BlockSpec((1, 1, blk), f, memory_space=pltpu.SMEM)
    grid_spec = pltpu.PrefetchScalarGridSpec(
        num_scalar_prefetch=2,
        grid=(n_blocks,),
        in_specs=[idx_spec(lambda i, be, nu: (i, 0, 0)),
                  idx_spec(lambda i, be, nu: (jnp.minimum(i + 1, last), 0, 0)),
                  idx_spec(lambda i, be, nu: (jnp.maximum(i - 1, 0), 0, 0)),
                  idx_spec(lambda i, be, nu: (i, 0, 0)),
                  pl.BlockSpec(memory_space=pl.ANY),
                  pl.BlockSpec((1, D_MODEL, 2 * D_EXPERT), lambda i, be, nu: (be[i], 0, 0)),
                  pl.BlockSpec((1, D_EXPERT, D_MODEL), lambda i, be, nu: (be[i], 0, 0))],
        out_specs=pl.BlockSpec(memory_space=pl.ANY),
        scratch_shapes=[pltpu.VMEM((2, blk, D_MODEL), F32), pltpu.VMEM((2, blk, D_MODEL), F32),
                        pltpu.SemaphoreType.DMA((2,)), pltpu.SemaphoreType.DMA((2,))],
    )
    return pl.pallas_call(
        functools.partial(_moe_kernel, blk=blk),
        grid_spec=grid_spec,
        out_shape=jax.ShapeDtypeStruct((P, D_MODEL), F32),
        compiler_params=_cparams(("arbitrary",)),
        name="moe",
    )(block_expert, n_used, tok3, tok3, dst3, dst3, u2d, pw["wgu"], pw["wd"])


def _final_kernel(h_ref, y1_ref, y2_ref, route_ref, g_ref, o_ref):
    r = route_ref[...]
    y = y1_ref[...] * r[:, 2:3] + y2_ref[...] * r[:, 3:4]
    o_ref[...] = _rms(h_ref[...] + y, g_ref[...])


def _final(h2d, y, route, gfin, tm):
    T = h2d.shape[0]
    second = T // tm
    return pl.pallas_call(
        _final_kernel,
        grid=(T // tm,),
        in_specs=[pl.BlockSpec((tm, D_MODEL), lambda i: (i, 0)),
                  pl.BlockSpec((tm, D_MODEL), lambda i: (i, 0)),
                  pl.BlockSpec((tm, D_MODEL), lambda i: (second + i, 0)),
                  pl.BlockSpec((tm, LANES), lambda i: (i, 0)),
                  _full(gfin.shape)],
        out_specs=pl.BlockSpec((tm, D_MODEL), lambda i: (i, 0)),
        out_shape=jax.ShapeDtypeStruct((T, D_MODEL), F32),
        compiler_params=_cparams(("parallel",)),
        name="final",
    )(h2d, y, y, route, gfin)


def _pair_swap(w):
    s = w.shape
    return w.reshape(s[:-1] + (s[-1] // 2, 2))[..., ::-1].reshape(s)


def _prep_weights(norm_mix_g, w_in, gqa_q_norm_g, gqa_k_norm_g, mla_q_norm_g, mla_kv_norm_g,
                  w_mla_q_up, w_mla_kv_up, gqa_out_norm_g, mla_out_norm_g, w_out, norm_ffn_g,
                  w_router_group, b_router_group, w_router_expert, b_router_expert,
                  w_expert_gate, w_expert_up, w_expert_down, norm_final_g):
    D = D_MODEL
    order = jnp.array(GQA_HEAD_ORDER)
    wq = w_in[:, :512].reshape(D, GQA_HEADS, GQA_HEAD_DIM)[:, order].reshape(D, GQA_WIDTH)
    wk = w_in[:, 512:640]
    wkr = w_in[:, 1152:1184]
    z64 = jnp.zeros((D, 64), F32)
    z32 = jnp.zeros((D, 32), F32)
    w1 = jnp.concatenate([
        wq, _pair_swap(wq), wk, _pair_swap(wk), w_in[:, 640:768], w_in[:, 768:1024],
        w_in[:, 1024:1152], jnp.concatenate([z64, wkr, z32], 1),
        jnp.concatenate([z64, _pair_swap(wkr), z32], 1)], axis=1).astype(BF16)

    qu = w_mla_q_up.reshape(MLA_Q_RANK, MLA_HEADS, MLA_QK_DIM)
    zq32 = jnp.zeros((MLA_Q_RANK, MLA_HEADS, 32), F32)
    zq64 = jnp.zeros((MLA_Q_RANK, MLA_HEADS, 64), F32)
    wqu = jnp.concatenate([qu, zq32], axis=2).reshape(MLA_Q_RANK, MLA_HEADS * LANES)
    wqus = jnp.concatenate([zq64, _pair_swap(qu[:, :, MLA_NOPE_DIM:]), zq32],
                           axis=2).reshape(MLA_Q_RANK, MLA_HEADS * LANES)
    kvu = w_mla_kv_up.reshape(MLA_KV_RANK, MLA_HEADS, MLA_NOPE_DIM + MLA_V_DIM)
    wku = jnp.concatenate([kvu[:, :, :MLA_NOPE_DIM],
                           jnp.zeros((MLA_KV_RANK, MLA_HEADS, 64), F32)],
                          axis=2).reshape(MLA_KV_RANK, MLA_HEADS * LANES)
    wvu = kvu[:, :, MLA_NOPE_DIM:].reshape(MLA_KV_RANK, MLA_WIDTH)

    head_block = jnp.full((GQA_HEAD_DIM, GQA_HEAD_DIM), 1.0 / GQA_HEAD_DIM, F32)
    perm_rows = lambda a: a.reshape((GQA_HEADS, GQA_HEAD_DIM) + a.shape[1:])[order].reshape(a.shape)
    wr = jnp.concatenate([w_router_group, w_router_expert,
                          jnp.zeros((D, LANES - N_GROUPS - N_EXPERTS), F32)], axis=1)
    wrh = wr.astype(BF16)
    br = jnp.concatenate([b_router_group, b_router_expert,
                          jnp.zeros((LANES - N_GROUPS - N_EXPERTS,), F32)])[None, :]
    return dict(
        gmix=norm_mix_g[None, :], w1=w1,
        gq=jnp.tile(gqa_q_norm_g, GQA_HEADS)[None, :],
        gqs=jnp.tile(_pair_swap(gqa_q_norm_g), GQA_HEADS)[None, :],
        gk=jnp.tile(gqa_k_norm_g, GQA_KV_HEADS)[None, :],
        gks=jnp.tile(_pair_swap(gqa_k_norm_g), GQA_KV_HEADS)[None, :],
        gcq=mla_q_norm_g[None, :], gckv=mla_kv_norm_g[None, :],
        wqu=wqu.astype(BF16), wqus=wqus.astype(BF16), wku=wku.astype(BF16),
        wvu=wvu.astype(BF16),
        bdq=jnp.kron(jnp.eye(GQA_HEADS, dtype=F32), head_block).astype(BF16),
        bdk=jnp.kron(jnp.eye(GQA_KV_HEADS, dtype=F32), head_block).astype(BF16),
        ga=perm_rows(gqa_out_norm_g)[None, :], gb=mla_out_norm_g[None, :],
        wout=jnp.concatenate([perm_rows(w_out[:GQA_WIDTH]), w_out[GQA_WIDTH:]], 0).astype(BF16),
        gffn=norm_ffn_g[None, :], wrh=wrh, wrl=(wr - wrh.astype(F32)).astype(BF16), br=br,
        wgu=jnp.concatenate([w_expert_gate, w_expert_up], axis=2).astype(BF16),
        wd=w_expert_down.astype(BF16), gfin=norm_final_g[None, :])


def _angles(rows, cols, rot_dim):
    half = rot_dim // 2
    inv = ROPE_THETA ** (-jnp.arange(0, half, 2, dtype=F32) / half)
    ang = jnp.concatenate([rows.astype(F32)[:, None] * inv,
                           cols.astype(F32)[:, None] * inv], axis=-1)
    return jnp.cos(ang), jnp.sin(ang)


def _rope_tables(rows, cols):
    L = rows.shape[0]
    lanes = lambda a: jnp.repeat(a, 2, axis=-1)
    sign = lambda w: jnp.tile(jnp.array([-1.0, 1.0], F32), w // 2)[None, :]
    cg, sg = _angles(rows, cols, GQA_HEAD_DIM)
    cg = jnp.tile(lanes(cg), (1, 2))
    sg = jnp.tile(lanes(sg) * sign(GQA_HEAD_DIM), (1, 2))
    cm, sm = _angles(rows, cols, MLA_ROPE_DIM)
    cm, sm = lanes(cm), lanes(sm) * sign(MLA_ROPE_DIM)
    one64, z64, z32 = jnp.ones((L, 64), F32), jnp.zeros((L, 64), F32), jnp.zeros((L, 32), F32)
    cmq = jnp.concatenate([one64, cm, z32], 1) * SCALE_B
    smq = jnp.concatenate([z64, sm, z32], 1) * SCALE_B
    cmk = jnp.concatenate([z64, cm, z32], 1)
    smk = jnp.concatenate([z64, sm, z32], 1)
    return (cg, sg, cmq, smq, cmk, smk)


def _tile_sizes(n):
    tm = min(256, n)
    tq_gqa = min(128, n)
    tq_mla = min(1024, n)
    tk = min(512, n // KV_UNROLL)
    return tm, (tq_gqa, min(4, n // tq_gqa)), (tq_mla, min(2, n // tq_mla)), tk


MOE_ROWS = 256
PLAN_CHUNK = 256


def _with_ones(vt, groups):
    lead, (rows, n) = vt.shape[:-2], vt.shape[-2:]
    g = vt.reshape(lead + (groups, rows // groups, n))
    ones = jnp.ones(lead + (groups, ONES_ROWS, n), vt.dtype)
    return jnp.concatenate([g, ones], axis=-2).reshape(lead + (rows + groups * ONES_ROWS, n))


def _trunk(x, pw, meta_kv, bias):
    B, n, D = x.shape
    T = B * n
    tm, gqa_tiles, mla_tiles, tk = _tile_sizes(n)
    t = jnp.arange(n, dtype=jnp.int32)
    tables = _rope_tables(t // GRID_W, t % GRID_W)
    x2d = x.reshape(T, D)
    qa, ka, va, qb, kb, vb = _project(x2d, tables, pw, tm)
    ka_m, vat_m, kb_m, vbt_m = meta_kv
    r3 = lambda a: a.reshape(B, n, a.shape[-1])
    t3 = lambda a: jnp.swapaxes(r3(a), 1, 2)
    oa = _gqa_attention(r3(qa), r3(ka), _with_ones(t3(va), 1), ka_m, vat_m, bias,
                        gqa_tiles[0], tk, gqa_tiles[1])
    ob = _mla_attention(r3(qb), r3(kb), _with_ones(t3(vb), MLA_HEADS), kb_m, vbt_m, bias,
                        mla_tiles[0], tk, mla_tiles[1])
    h, u, route = _post(oa.reshape(T, GQA_WIDTH), ob.reshape(T, MLA_WIDTH), x2d, pw, tm)
    eid = route[:, :2].astype(jnp.int32)
    y = _moe(u, eid, pw, MOE_ROWS)
    out = _final(h, y, route, pw["gfin"], tm)
    return out.reshape(B, n, D)


def kernel(x_prompt, x_sample, meta_tokens, norm_mix_g, w_in, gqa_q_norm_g, gqa_k_norm_g, mla_q_norm_g, mla_kv_norm_g, w_mla_q_up, w_mla_kv_up, gqa_out_norm_g, mla_out_norm_g, w_out, norm_ffn_g, w_router_group, b_router_group, w_router_expert, b_router_expert, w_expert_gate, w_expert_up, w_expert_down, norm_final_g):
    pw = _prep_weights(norm_mix_g[0], w_in[0], gqa_q_norm_g[0], gqa_k_norm_g[0],
                       mla_q_norm_g[0], mla_kv_norm_g[0], w_mla_q_up[0], w_mla_kv_up[0],
                       gqa_out_norm_g[0], mla_out_norm_g[0], w_out[0], norm_ffn_g[0],
                       w_router_group[0], b_router_group[0], w_router_expert[0],
                       b_router_expert[0], w_expert_gate[0], w_expert_up[0],
                       w_expert_down[0], norm_final_g)
    meta_x = jnp.concatenate([meta_tokens.astype(F32),
                              jnp.zeros((META_PAD - N_META, D_MODEL), F32)], axis=0)
    mpos = jnp.arange(META_PAD, dtype=jnp.int32)
    meta_tables = _rope_tables(jnp.full((META_PAD,), -1, jnp.int32),
                               jnp.where(mpos < N_META, mpos, 0))
    _, ka_m, va_m, _, kb_m, vb_m = _project(meta_x, meta_tables, pw, META_PAD)
    bias = jnp.where(mpos < N_META, 0.0, NEG_BIG).astype(F32)[:, None]
    meta_kv = (ka_m, _with_ones(va_m.T, 1), kb_m, _with_ones(vb_m.T, MLA_HEADS))
    return (_trunk(x_prompt, pw, meta_kv, bias), _trunk(x_sample, pw, meta_kv, bias))
```

```python
import functools
import math

import jax
import jax.numpy as jnp
from jax import lax
from jax.experimental import pallas as pl
from jax.experimental.pallas import tpu as pltpu

F32 = jnp.float32
BF16 = jnp.bfloat16

D_MODEL = 1024
N_META = 16
GRID_W = 64
ROPE_THETA = 10000.0
NORM_EPS = 1e-6

GQA_HEADS = 8
GQA_KV_HEADS = 2
GQA_HEAD_DIM = 64
GQA_WIDTH = GQA_HEADS * GQA_HEAD_DIM

MLA_HEADS = 8
MLA_Q_RANK = 256
MLA_KV_RANK = 128
MLA_NOPE_DIM = 64
MLA_ROPE_DIM = 32
MLA_V_DIM = 64
MLA_QK_DIM = MLA_NOPE_DIM + MLA_ROPE_DIM
MLA_WIDTH = MLA_HEADS * MLA_V_DIM

N_GROUPS = 4
EXPERTS_PER_GROUP = 8
N_EXPERTS = N_GROUPS * EXPERTS_PER_GROUP
D_EXPERT = 256

LANES = 128
HALF = LANES // 2
META_PAD = 128
ONES_ROWS = 16
MLA_VROWS = MLA_V_DIM + ONES_ROWS
KV_UNROLL = 16
NEG_BIG = -1e30
VMEM_LIMIT = 56 * 1024 * 1024

LOG2E = 1.4426950408889634
SCALE_A = GQA_HEAD_DIM ** -0.5 * LOG2E
SCALE_B = MLA_QK_DIM ** -0.5 * LOG2E

GQA_HEAD_ORDER = (0, 4, 1, 5, 2, 6, 3, 7)

_C_Q, _C_QS, _C_K, _C_KS, _C_V, _C_CQ, _C_CKV, _C_KR, _C_KRS, _C_END = (
    0, 512, 1024, 1152, 1280, 1408, 1664, 1792, 1920, 2048)

NT_DIMS = (((1,), (1,)), ((), ()))


def _cparams(sem):
    return pltpu.CompilerParams(dimension_semantics=sem, vmem_limit_bytes=VMEM_LIMIT)


def _full(shape):
    nd = len(shape)
    return pl.BlockSpec(shape, lambda *_: (0,) * nd)


def _rms(x, g):
    return x * lax.rsqrt(jnp.mean(x * x, axis=-1, keepdims=True) + NORM_EPS) * g


def _tile_lanes(a, reps):
    return jnp.concatenate([a] * reps, axis=1)


def _head_mean_sq(z, bd):
    sq = z * z
    hi = sq.astype(BF16)
    lo = (sq - hi.astype(F32)).astype(BF16)
    return (jnp.dot(hi, bd, preferred_element_type=F32)
            + jnp.dot(lo, bd, preferred_element_type=F32))


def _proj_kernel(x_ref, cg_ref, sg_ref, cmq_ref, smq_ref, cmk_ref, smk_ref,
                 gmix_ref, w1_ref, gq_ref, gqs_ref, gk_ref, gks_ref, gcq_ref, gckv_ref,
                 wqu_ref, wqus_ref, wku_ref, wvu_ref, bdq_ref, bdk_ref,
                 qa_ref, ka_ref, va_ref, qb_ref, kb_ref, vb_ref):
    u = _rms(x_ref[...], gmix_ref[...]).astype(BF16)
    z = jnp.dot(u, w1_ref[...], preferred_element_type=F32)

    cg, sg = cg_ref[...], sg_ref[...]
    zq, zqs = z[:, _C_Q:_C_QS], z[:, _C_QS:_C_K]
    rq = lax.rsqrt(_head_mean_sq(zq, bdq_ref[...]) + NORM_EPS) * SCALE_A
    c4, s4 = _tile_lanes(cg, 4), _tile_lanes(sg, 4)
    qa_ref[...] = (rq * (zq * gq_ref[...] * c4 + zqs * gqs_ref[...] * s4)).astype(BF16)
    zk, zks = z[:, _C_K:_C_KS], z[:, _C_KS:_C_V]
    rk = lax.rsqrt(_head_mean_sq(zk, bdk_ref[...]) + NORM_EPS)
    ka_ref[...] = (rk * (zk * gk_ref[...] * cg + zks * gks_ref[...] * sg)).astype(BF16)
    va_ref[...] = z[:, _C_V:_C_CQ].astype(BF16)

    cqn = _rms(z[:, _C_CQ:_C_CKV], gcq_ref[...]).astype(BF16)
    zqb = jnp.dot(cqn, wqu_ref[...], preferred_element_type=F32)
    zqbs = jnp.dot(cqn, wqus_ref[...], preferred_element_type=F32)
    qb_ref[...] = (zqb * _tile_lanes(cmq_ref[...], MLA_HEADS)
                   + zqbs * _tile_lanes(smq_ref[...], MLA_HEADS)).astype(BF16)
    ckvn = _rms(z[:, _C_CKV:_C_KR], gckv_ref[...]).astype(BF16)
    kn = jnp.dot(ckvn, wku_ref[...], preferred_element_type=F32)
    kr = z[:, _C_KR:_C_KRS] * cmk_ref[...] + z[:, _C_KRS:_C_END] * smk_ref[...]
    kb_ref[...] = (kn + _tile_lanes(kr, MLA_HEADS)).astype(BF16)
    vb_ref[...] = jnp.dot(ckvn, wvu_ref[...], preferred_element_type=F32).astype(BF16)


def _project(x2d, tables, pw, tm):
    T = x2d.shape[0]
    n_pos = tables[0].shape[0]
    pos_blocks = n_pos // tm
    row = lambda w: pl.BlockSpec((tm, w), lambda i: (i, 0))
    tab = pl.BlockSpec((tm, LANES), lambda i: (i % pos_blocks, 0))
    weights = (pw["gmix"], pw["w1"], pw["gq"], pw["gqs"], pw["gk"], pw["gks"], pw["gcq"],
               pw["gckv"], pw["wqu"], pw["wqus"], pw["wku"], pw["wvu"], pw["bdq"], pw["bdk"])
    widths = (GQA_WIDTH, LANES, LANES, MLA_HEADS * LANES, MLA_HEADS * LANES, MLA_WIDTH)
    return pl.pallas_call(
        _proj_kernel,
        grid=(T // tm,),
        in_specs=[row(D_MODEL)] + [tab] * 6 + [_full(w.shape) for w in weights],
        out_specs=[row(w) for w in widths],
        out_shape=[jax.ShapeDtypeStruct((T, w), BF16) for w in widths],
        compiler_params=_cparams(("parallel",)),
        name="proj",
    )(x2d, *tables, *weights)


def _flash(load_q, load_k, load_vt, km, vmt, bias, n_kv, tk, vdim, acc_ref, s_refs):
    assert n_kv >= KV_UNROLL and n_kv % KV_UNROLL == 0 and KV_UNROLL % 2 == 0

    def scores(j):
        off = pl.multiple_of(j * tk, tk)
        return lax.dot_general(load_k(off), load_q(), NT_DIMS, preferred_element_type=F32)

    def consume(s, j, m_prev):
        off = pl.multiple_of(j * tk, tk)
        m_new = jnp.maximum(m_prev, jnp.max(s, axis=0, keepdims=True))
        alpha = jnp.exp2(m_prev - m_new)
        p = jnp.exp2(s - m_new)
        acc_ref[...] = alpha * acc_ref[...] + jnp.dot(
            load_vt(off), p.astype(BF16), preferred_element_type=F32)
        return m_new

    s = lax.dot_general(km, load_q(), NT_DIMS, preferred_element_type=F32) + bias
    m0 = jnp.max(s, axis=0, keepdims=True)
    acc_ref[...] = jnp.dot(vmt, jnp.exp2(s - m0).astype(BF16), preferred_element_type=F32)
    s_refs[0][...] = scores(0)

    def group(i, m, last):
        j0 = KV_UNROLL * i
        for u in range(KV_UNROLL):
            if not (last and u == KV_UNROLL - 1):
                s_refs[(u + 1) % 2][...] = scores(j0 + u + 1)
            m = consume(s_refs[u % 2][...], j0 + u, m)
        return m

    trips = n_kv // KV_UNROLL
    m = lax.fori_loop(0, trips - 1, lambda i, c: group(i, c, False), m0)
    group(trips - 1, m, True)
    acc = acc_ref[...]
    return acc[:vdim] * (1.0 / acc[vdim:vdim + 1])


def _gqa_kernel(q_ref, k_ref, vt_ref, km_ref, vmt_ref, bias_ref, o_ref,
                qs_ref, acc_ref, s0_ref, s1_ref, *, tq, tk, n_kv, nsub):
    lo = lax.broadcasted_iota(jnp.int32, (tq, LANES), 1) < HALF
    zero = jnp.zeros((tq, LANES), BF16)

    def sub_tile(i, carry):
        rows = pl.ds(pl.multiple_of(i * tq, tq), tq)
        for c in range(GQA_HEADS // 2):
            qc = q_ref[0, rows, c * LANES:(c + 1) * LANES]
            qs_ref[(2 * c) * tq:(2 * c + 1) * tq, :] = jnp.where(lo, qc, zero)
            qs_ref[(2 * c + 1) * tq:(2 * c + 2) * tq, :] = jnp.where(lo, zero, qc)
        ot = _flash(lambda: qs_ref[...],
                    lambda off: k_ref[0, pl.ds(off, tk), :],
                    lambda off: vt_ref[0, :, pl.ds(off, tk)],
                    km_ref[...], vmt_ref[...], bias_ref[...], n_kv, tk, LANES,
                    acc_ref, (s0_ref, s1_ref))
        for c in range(GQA_HEADS // 2):
            pair = jnp.concatenate([ot[:HALF, (2 * c) * tq:(2 * c + 1) * tq],
                                    ot[HALF:, (2 * c + 1) * tq:(2 * c + 2) * tq]], axis=0)
            o_ref[0, rows, c * LANES:(c + 1) * LANES] = pair.T
        return carry

    lax.fori_loop(0, nsub, sub_tile, 0)


def _gqa_attention(qa, ka, vat, km, vmt, bias, tq, tk, nsub):
    B, n, _ = qa.shape
    R = GQA_HEADS * tq
    tb = tq * nsub
    kern = functools.partial(_gqa_kernel, tq=tq, tk=tk, n_kv=n // tk, nsub=nsub)
    return pl.pallas_call(
        kern,
        grid=(B, n // tb),
        in_specs=[pl.BlockSpec((1, tb, GQA_WIDTH), lambda b, i: (b, i, 0)),
                  pl.BlockSpec((1, n, LANES), lambda b, i: (b, 0, 0)),
                  pl.BlockSpec((1, LANES + ONES_ROWS, n), lambda b, i: (b, 0, 0)),
                  _full(km.shape), _full(vmt.shape), _full(bias.shape)],
        out_specs=pl.BlockSpec((1, tb, GQA_WIDTH), lambda b, i: (b, i, 0)),
        out_shape=jax.ShapeDtypeStruct((B, n, GQA_WIDTH), F32),
        scratch_shapes=[pltpu.VMEM((R, LANES), BF16), pltpu.VMEM((LANES + ONES_ROWS, R), F32),
                        pltpu.VMEM((tk, R), F32), pltpu.VMEM((tk, R), F32)],
        compiler_params=_cparams(("parallel", "arbitrary")),
        name="gqa_attn",
    )(qa, ka, vat, km, vmt, bias)


def _mla_kernel(q_ref, k_ref, vt_ref, km_ref, vmt_ref, bias_ref, o_ref, acc_ref,
                s0_ref, s1_ref, *, tq, tk, n_kv, nsub):
    def sub_tile(i, carry):
        rows = pl.ds(pl.multiple_of(i * tq, tq), tq)
        outs = []
        for h in range(2):
            sl = slice(h * LANES, (h + 1) * LANES)
            vs = slice(h * MLA_VROWS, (h + 1) * MLA_VROWS)
            outs.append(_flash(lambda: q_ref[0, rows, sl],
                               lambda off: k_ref[0, pl.ds(off, tk), sl],
                               lambda off: vt_ref[0, vs, pl.ds(off, tk)],
                               km_ref[:, sl], vmt_ref[vs, :], bias_ref[...], n_kv, tk,
                               MLA_V_DIM, acc_ref, (s0_ref, s1_ref)))
        o_ref[0, rows, :] = jnp.concatenate(outs, axis=0).T
        return carry

    lax.fori_loop(0, nsub, sub_tile, 0)


def _mla_attention(qb, kb, vbt, km, vmt, bias, tq, tk, nsub):
    B, n, _ = qb.shape
    pairs = MLA_HEADS // 2
    tb = tq * nsub
    kern = functools.partial(_mla_kernel, tq=tq, tk=tk, n_kv=n // tk, nsub=nsub)
    return pl.pallas_call(
        kern,
        grid=(B, pairs, n // tb),
        in_specs=[pl.BlockSpec((1, tb, 2 * LANES), lambda b, j, i: (b, i, j)),
                  pl.BlockSpec((1, n, 2 * LANES), lambda b, j, i: (b, 0, j)),
                  pl.BlockSpec((1, 2 * MLA_VROWS, n), lambda b, j, i: (b, j, 0)),
                  pl.BlockSpec((META_PAD, 2 * LANES), lambda b, j, i: (0, j)),
                  pl.BlockSpec((2 * MLA_VROWS, META_PAD), lambda b, j, i: (j, 0)),
                  _full(bias.shape)],
        out_specs=pl.BlockSpec((1, tb, LANES), lambda b, j, i: (b, i, j)),
        out_shape=jax.ShapeDtypeStruct((B, n, MLA_WIDTH), F32),
        scratch_shapes=[pltpu.VMEM((MLA_VROWS, tq), F32), pltpu.VMEM((tk, tq), F32),
                        pltpu.VMEM((tk, tq), F32)],
        compiler_params=_cparams(("parallel", "parallel", "arbitrary")),
        name="mla_attn",
    )(qb, kb, vbt, km, vmt, bias)


def _post_kernel(oa_ref, ob_ref, x_ref, ga_ref, gb_ref, wout_ref, gffn_ref,
                 wrh_ref, wrl_ref, br_ref, h_ref, u_ref, route_ref):
    o = jnp.concatenate([_rms(oa_ref[...], ga_ref[...]), _rms(ob_ref[...], gb_ref[...])],
                        axis=1).astype(BF16)
    h = x_ref[...] + jnp.dot(o, wout_ref[...], preferred_element_type=F32)
    h_ref[...] = h
    u = _rms(h, gffn_ref[...])
    u_ref[...] = u

    u_hi = u.astype(BF16)
    u_lo = (u - u_hi.astype(F32)).astype(BF16)
    wrh = wrh_ref[...]
    logits = (jnp.dot(u_hi, wrh, preferred_element_type=F32)
              + jnp.dot(u_lo, wrh, preferred_element_type=F32)
              + jnp.dot(u_hi, wrl_ref[...], preferred_element_type=F32)) + br_ref[...]

    lane = lax.broadcasted_iota(jnp.int32, logits.shape, 1)
    far = jnp.int32(LANES)
    neg = jnp.float32(-jnp.inf)
    lg = jnp.where(lane < N_GROUPS, logits, neg)
    mg = jnp.max(lg, axis=1, keepdims=True)
    p_grp = 1.0 / jnp.sum(jnp.exp(lg - mg), axis=1, keepdims=True)
    grp = jnp.min(jnp.where(lg == mg, lane, far), axis=1, keepdims=True)
    e_lo = N_GROUPS + grp * EXPERTS_PER_GROUP
    le = jnp.where((lane >= e_lo) & (lane < e_lo + EXPERTS_PER_GROUP), logits, neg)
    m1 = jnp.max(le, axis=1, keepdims=True)
    i1 = jnp.min(jnp.where(le == m1, lane, far), axis=1, keepdims=True)
    le2 = jnp.where(lane == i1, neg, le)
    m2 = jnp.max(le2, axis=1, keepdims=True)
    i2 = jnp.min(jnp.where(le2 == m2, lane, far), axis=1, keepdims=True)
    t = jnp.exp(m2 - m1)
    g1 = p_grp / (1.0 + t)
    g2 = g1 * t
    e1 = (i1 - N_GROUPS).astype(F32)
    e2 = (i2 - N_GROUPS).astype(F32)
    route_ref[...] = jnp.where(lane == 0, e1, jnp.where(lane == 1, e2, jnp.where(
        lane == 2, g1, jnp.where(lane == 3, g2, 0.0))))


def _post(oa, ob, x2d, pw, tm):
    T = x2d.shape[0]
    row = lambda w: pl.BlockSpec((tm, w), lambda i: (i, 0))
    weights = (pw["ga"], pw["gb"], pw["wout"], pw["gffn"], pw["wrh"], pw["wrl"], pw["br"])
    return pl.pallas_call(
        _post_kernel,
        grid=(T // tm,),
        in_specs=[row(GQA_WIDTH), row(MLA_WIDTH), row(D_MODEL)]
        + [_full(w.shape) for w in weights],
        out_specs=[row(D_MODEL), row(D_MODEL), row(LANES)],
        out_shape=[jax.ShapeDtypeStruct((T, D_MODEL), F32),
                   jax.ShapeDtypeStruct((T, D_MODEL), F32),
                   jax.ShapeDtypeStruct((T, LANES), F32)],
        compiler_params=_cparams(("parallel",)),
        name="post",
    )(oa, ob, x2d, *weights)


def _moe_kernel(be_ref, nused_ref, tok_ref, tokn_ref, dstp_ref, dst_ref, x_hbm, wgu_ref, wd_ref,
                y_hbm, xbuf, ybuf, gsem, ssem, *, blk):
    i = pl.program_id(0)
    nb = pl.num_programs(0)
    slot = i % 2
    n_used = nused_ref[0]

    def gather_copy(t, r, s):
        return pltpu.make_async_copy(x_hbm.at[pl.ds(t, 1), :], xbuf.at[s, pl.ds(r, 1), :],
                                     gsem.at[s])

    def scatter_copy(d, r, s):
        return pltpu.make_async_copy(ybuf.at[s, pl.ds(r, 1), :], y_hbm.at[pl.ds(d, 1), :],
                                     ssem.at[s])

    def start_rows(make, idx_ref, s, unrolled):
        if unrolled:
            for r in range(blk):
                make(idx_ref[0, 0, r], r, s).start(priority=r % 2)
        else:
            def body(r, c):
                make(idx_ref[0, 0, r], r, s).start()
                return c
            lax.fori_loop(0, blk, body, 0)

    def wait_rows(make, s):
        def body(r, c):
            make(0, r, s).wait()
            return c
        lax.fori_loop(0, blk, body, 0, unroll=8)

    def experts(s):
        xe = xbuf[s].astype(BF16)
        gu = jnp.dot(xe, wgu_ref[0], preferred_element_type=F32)
        g, up = gu[:, :D_EXPERT], gu[:, D_EXPERT:]
        hid = (g * jax.nn.sigmoid(g) * up).astype(BF16)
        ybuf[s] = jnp.dot(hid, wd_ref[0], preferred_element_type=F32)

    @pl.when(i == 0)
    def _():
        start_rows(gather_copy, tok_ref, 0, False)

    wait_rows(gather_copy, slot)

    @pl.when(i >= 2)
    def _():
        wait_rows(scatter_copy, slot)

    @pl.when(i == 0)
    def _():
        start_rows(gather_copy, tokn_ref, 1 - slot, False)
        experts(slot)

    for par in range(2):
        @pl.when((i >= 1) & (i < n_used) & (slot == par))
        def _():
            start_rows(gather_copy, tokn_ref, 1 - par, True)
            experts(par)
            start_rows(scatter_copy, dstp_ref, 1 - par, True)

    @pl.when((i >= 1) & (i >= n_used))
    def _():
        start_rows(gather_copy, tokn_ref, 1 - slot, False)
        ybuf[slot] = jnp.zeros((blk, D_MODEL), F32)
        start_rows(scatter_copy, dstp_ref, 1 - slot, False)

    @pl.when(i == nb - 1)
    def _():
        start_rows(scatter_copy, dst_ref, slot, False)
        wait_rows(gather_copy, 1 - slot)

        @pl.when(i >= 1)
        def _():
            wait_rows(scatter_copy, 1 - slot)
        wait_rows(scatter_copy, slot)


def _moe_plan(eid, blk):
    T = eid.shape[0]
    A = 2 * T
    n_blocks = (A + N_EXPERTS * (blk - 1) + blk - 1) // blk
    P = n_blocks * blk
    e_flat = eid.T.reshape(A)
    onehot = (e_flat[:, None] == jnp.arange(N_EXPERTS, dtype=jnp.int32)[None, :]).astype(jnp.int32)
    chunks = A // PLAN_CHUNK
    tri = jnp.tril(jnp.ones((PLAN_CHUNK, PLAN_CHUNK), BF16))
    within = jnp.einsum("ij,cjk->cik", tri,
                        onehot.astype(BF16).reshape(chunks, PLAN_CHUNK, N_EXPERTS),
                        preferred_element_type=F32).astype(jnp.int32)
    totals = within[:, -1, :]
    csum = (within + (jnp.cumsum(totals, axis=0) - totals)[:, None, :]).reshape(A, N_EXPERTS)
    rank = jnp.sum(csum * onehot, axis=1) - 1
    counts = csum[-1]
    padded = (counts + blk - 1) // blk * blk
    pends = jnp.cumsum(padded)
    pstarts = pends - padded
    slot = jnp.sum(onehot * pstarts[None, :], axis=1) + rank
    a_idx = jnp.arange(A, dtype=jnp.int32)
    buf_a = jnp.full((P,), -1, jnp.int32).at[slot].set(
        a_idx, unique_indices=True, mode="promise_in_bounds")
    invalid = buf_a < 0
    buf_tok = jnp.where(invalid, 0, buf_a % T)
    block_start = jnp.arange(n_blocks, dtype=jnp.int32) * blk
    block_expert = jnp.minimum(
        jnp.sum((pends[None, :] <= block_start[:, None]).astype(jnp.int32), axis=1),
        N_EXPERTS - 1)
    pad_cnt = padded - counts
    pad_before = jnp.cumsum(pad_cnt) - pad_cnt
    block_base = (pad_before - pstarts - counts)[block_expert]
    s_idx = jnp.arange(P, dtype=jnp.int32)
    pad_row = jnp.where(s_idx < pends[-1], A + jnp.repeat(block_base, blk) + s_idx, s_idx)
    buf_dst = jnp.where(invalid, pad_row, buf_a)
    n_used = (pends[-1] // blk).astype(jnp.int32).reshape(1)
    return block_expert, n_used, buf_tok, buf_dst, n_blocks


def _moe(u2d, eid, pw, blk):
    block_expert, n_used, buf_tok, buf_dst, n_blocks = _moe_plan(eid, blk)
    P = n_blocks * blk
    tok3 = buf_tok.reshape(n_blocks, 1, blk)
    dst3 = buf_dst.reshape(n_blocks, 1, blk)
    last = n_blocks - 1
    idx_spec = lambda f: pl.BlockSpec((1, 1, blk), f, memory_space=pltpu.SMEM)
    grid_spec = pltpu.PrefetchScalarGridSpec(
        num_scalar_prefetch=2,
        grid=(n_blocks,),
        in_specs=[idx_spec(lambda i, be, nu: (i, 0, 0)),
                  idx_spec(lambda i, be, nu: (jnp.minimum(i + 1, last), 0, 0)),
                  idx_spec(lambda i, be, nu: (jnp.maximum(i - 1, 0), 0, 0)),
                  idx_spec(lambda i, be, nu: (i, 0, 0)),
                  pl.BlockSpec(memory_space=pl.ANY),
                  pl.BlockSpec((1, D_MODEL, 2 * D_EXPERT), lambda i, be, nu: (be[i], 0, 0)),
                  pl.BlockSpec((1, D_EXPERT, D_MODEL), lambda i, be, nu: (be[i], 0, 0))],
        out_specs=pl.BlockSpec(memory_space=pl.ANY),
        scratch_shapes=[pltpu.VMEM((2, blk, D_MODEL), F32), pltpu.VMEM((2, blk, D_MODEL), F32),
                        pltpu.SemaphoreType.DMA((2,)), pltpu.SemaphoreType.DMA((2,))],
    )
    return pl.pallas_call(
        functools.partial(_moe_kernel, blk=blk),
        grid_spec=grid_spec,
        out_shape=jax.ShapeDtypeStruct((P, D_MODEL), F32),
        compiler_params=_cparams(("arbitrary",)),
        name="moe",
    )(block_expert, n_used, tok3, tok3, dst3, dst3, u2d, pw["wgu"], pw["wd"])


def _final_kernel(h_ref, y1_ref, y2_ref, route_ref, g_ref, o_ref):
    r = route_ref[...]
    y = y1_ref[...] * r[:, 2:3] + y2_ref[...] * r[:, 3:4]
    o_ref[...] = _rms(h_ref[...] + y, g_ref[...])


def _final(h2d, y, route, gfin, tm):
    T = h2d.shape[0]
    second = T // tm
    return pl.pallas_call(
        _final_kernel,
        grid=(T // tm,),
        in_specs=[pl.BlockSpec((tm, D_MODEL), lambda i: (i, 0)),
                  pl.BlockSpec((tm, D_MODEL), lambda i: (i, 0)),
                  pl.BlockSpec((tm, D_MODEL), lambda i: (second + i, 0)),
                  pl.BlockSpec((tm, LANES), lambda i: (i, 0)),
                  _full(gfin.shape)],
        out_specs=pl.BlockSpec((tm, D_MODEL), lambda i: (i, 0)),
        out_shape=jax.ShapeDtypeStruct((T, D_MODEL), F32),
        compiler_params=_cparams(("parallel",)),
        name="final",
    )(h2d, y, y, route, gfin)


def _pair_swap(w):
    s = w.shape
    return w.reshape(s[:-1] + (s[-1] // 2, 2))[..., ::-1].reshape(s)


def _prep_weights(norm_mix_g, w_in, gqa_q_norm_g, gqa_k_norm_g, mla_q_norm_g, mla_kv_norm_g,
                  w_mla_q_up, w_mla_kv_up, gqa_out_norm_g, mla_out_norm_g, w_out, norm_ffn_g,
                  w_router_group, b_router_group, w_router_expert, b_router_expert,
                  w_expert_gate, w_expert_up, w_expert_down, norm_final_g):
    D = D_MODEL
    order = jnp.array(GQA_HEAD_ORDER)
    wq = w_in[:, :512].reshape(D, GQA_HEADS, GQA_HEAD_DIM)[:, order].reshape(D, GQA_WIDTH)
    wk = w_in[:, 512:640]
    wkr = w_in[:, 1152:1184]
    z64 = jnp.zeros((D, 64), F32)
    z32 = jnp.zeros((D, 32), F32)
    w1 = jnp.concatenate([
        wq, _pair_swap(wq), wk, _pair_swap(wk), w_in[:, 640:768], w_in[:, 768:1024],
        w_in[:, 1024:1152], jnp.concatenate([z64, wkr, z32], 1),
        jnp.concatenate([z64, _pair_swap(wkr), z32], 1)], axis=1).astype(BF16)

    qu = w_mla_q_up.reshape(MLA_Q_RANK, MLA_HEADS, MLA_QK_DIM)
    zq32 = jnp.zeros((MLA_Q_RANK, MLA_HEADS, 32), F32)
    zq64 = jnp.zeros((MLA_Q_RANK, MLA_HEADS, 64), F32)
    wqu = jnp.concatenate([qu, zq32], axis=2).reshape(MLA_Q_RANK, MLA_HEADS * LANES)
    wqus = jnp.concatenate([zq64, _pair_swap(qu[:, :, MLA_NOPE_DIM:]), zq32],
                           axis=2).reshape(MLA_Q_RANK, MLA_HEADS * LANES)
    kvu = w_mla_kv_up.reshape(MLA_KV_RANK, MLA_HEADS, MLA_NOPE_DIM + MLA_V_DIM)
    wku = jnp.concatenate([kvu[:, :, :MLA_NOPE_DIM],
                           jnp.zeros((MLA_KV_RANK, MLA_HEADS, 64), F32)],
                          axis=2).reshape(MLA_KV_RANK, MLA_HEADS * LANES)
    wvu = kvu[:, :, MLA_NOPE_DIM:].reshape(MLA_KV_RANK, MLA_WIDTH)

    head_block = jnp.full((GQA_HEAD_DIM, GQA_HEAD_DIM), 1.0 / GQA_HEAD_DIM, F32)
    perm_rows = lambda a: a.reshape((GQA_HEADS, GQA_HEAD_DIM) + a.shape[1:])[order].reshape(a.shape)
    wr = jnp.concatenate([w_router_group, w_router_expert,
                          jnp.zeros((D, LANES - N_GROUPS - N_EXPERTS), F32)], axis=1)
    wrh = wr.astype(BF16)
    br = jnp.concatenate([b_router_group, b_router_expert,
                          jnp.zeros((LANES - N_GROUPS - N_EXPERTS,), F32)])[None, :]
    return dict(
        gmix=norm_mix_g[None, :], w1=w1,
        gq=jnp.tile(gqa_q_norm_g, GQA_HEADS)[None, :],
        gqs=jnp.tile(_pair_swap(gqa_q_norm_g), GQA_HEADS)[None, :],
        gk=jnp.tile(gqa_k_norm_g, GQA_KV_HEADS)[None, :],
        gks=jnp.tile(_pair_swap(gqa_k_norm_g), GQA_KV_HEADS)[None, :],
        gcq=mla_q_norm_g[None, :], gckv=mla_kv_norm_g[None, :],
        wqu=wqu.astype(BF16), wqus=wqus.astype(BF16), wku=wku.astype(BF16),
        wvu=wvu.astype(BF16),
        bdq=jnp.kron(jnp.eye(GQA_HEADS, dtype=F32), head_block).astype(BF16),
        bdk=jnp.kron(jnp.eye(GQA_KV_HEADS, dtype=F32), head_block).astype(BF16),
        ga=perm_rows(gqa_out_norm_g)[None, :], gb=mla_out_norm_g[None, :],
        wout=jnp.concatenate([perm_rows(w_out[:GQA_WIDTH]), w_out[GQA_WIDTH:]], 0).astype(BF16),
        gffn=norm_ffn_g[None, :], wrh=wrh, wrl=(wr - wrh.astype(F32)).astype(BF16), br=br,
        wgu=jnp.concatenate([w_expert_gate, w_expert_up], axis=2).astype(BF16),
        wd=w_expert_down.astype(BF16), gfin=norm_final_g[None, :])


def _angles(rows, cols, rot_dim):
    half = rot_dim // 2
    inv = ROPE_THETA ** (-jnp.arange(0, half, 2, dtype=F32) / half)
    ang = jnp.concatenate([rows.astype(F32)[:, None] * inv,
                           cols.astype(F32)[:, None] * inv], axis=-1)
    return jnp.cos(ang), jnp.sin(ang)


def _rope_tables(rows, cols):
    L = rows.shape[0]
    lanes = lambda a: jnp.repeat(a, 2, axis=-1)
    sign = lambda w: jnp.tile(jnp.array([-1.0, 1.0], F32), w // 2)[None, :]
    cg, sg = _angles(rows, cols, GQA_HEAD_DIM)
    cg = jnp.tile(lanes(cg), (1, 2))
    sg = jnp.tile(lanes(sg) * sign(GQA_HEAD_DIM), (1, 2))
    cm, sm = _angles(rows, cols, MLA_ROPE_DIM)
    cm, sm = lanes(cm), lanes(sm) * sign(MLA_ROPE_DIM)
    one64, z64, z32 = jnp.ones((L, 64), F32), jnp.zeros((L, 64), F32), jnp.zeros((L, 32), F32)
    cmq = jnp.concatenate([one64, cm, z32], 1) * SCALE_B
    smq = jnp.concatenate([z64, sm, z32], 1) * SCALE_B
    cmk = jnp.concatenate([z64, cm, z32], 1)
    smk = jnp.concatenate([z64, sm, z32], 1)
    return (cg, sg, cmq, smq, cmk, smk)


def _tile_sizes(n):
    tm = min(256, n)
    tq_gqa = min(128, n)
    tq_mla = min(1024, n)
    tk = min(512, n // KV_UNROLL)
    return tm, (tq_gqa, min(4, n // tq_gqa)), (tq_mla, min(2, n // tq_mla)), tk


MOE_ROWS = 256
PLAN_CHUNK = 256


def _with_ones(vt, groups):
    lead, (rows, n) = vt.shape[:-2], vt.shape[-2:]
    g = vt.reshape(lead + (groups, rows // groups, n))
    ones = jnp.ones(lead + (groups, ONES_ROWS, n), vt.dtype)
    return jnp.concatenate([g, ones], axis=-2).reshape(lead + (rows + groups * ONES_ROWS, n))


def _trunk(x, pw, meta_kv, bias):
    B, n, D = x.shape
    T = B * n
    tm, gqa_tiles, mla_tiles, tk = _tile_sizes(n)
    t = jnp.arange(n, dtype=jnp.int32)
    tables = _rope_tables(t // GRID_W, t % GRID_W)
    x2d = x.reshape(T, D)
    qa, ka, va, qb, kb, vb = _project(x2d, tables, pw, tm)
    ka_m, vat_m, kb_m, vbt_m = meta_kv
    r3 = lambda a: a.reshape(B, n, a.shape[-1])
    t3 = lambda a: jnp.swapaxes(r3(a), 1, 2)
    oa = _gqa_attention(r3(qa), r3(ka), _with_ones(t3(va), 1), ka_m, vat_m, bias,
                        gqa_tiles[0], tk, gqa_tiles[1])
    ob = _mla_attention(r3(qb), r3(kb), _with_ones(t3(vb), MLA_HEADS), kb_m, vbt_m, bias,
                        mla_tiles[0], tk, mla_tiles[1])
    h, u, route = _post(oa.reshape(T, GQA_WIDTH), ob.reshape(T, MLA_WIDTH), x2d, pw, tm)
    eid = route[:, :2].astype(jnp.int32)
    y = _moe(u, eid, pw, MOE_ROWS)
    out = _final(h, y, route, pw["gfin"], tm)
    return out.reshape(B, n, D)


def kernel(x_prompt, x_sample, meta_tokens, norm_mix_g, w_in, gqa_q_norm_g, gqa_k_norm_g, mla_q_norm_g, mla_kv_norm_g, w_mla_q_up, w_mla_kv_up, gqa_out_norm_g, mla_out_norm_g, w_out, norm_ffn_g, w_router_group, b_router_group, w_router_expert, b_router_expert, w_expert_gate, w_expert_up, w_expert_down, norm_final_g):
    pw = _prep_weights(norm_mix_g[0], w_in[0], gqa_q_norm_g[0], gqa_k_norm_g[0],
                       mla_q_norm_g[0], mla_kv_norm_g[0], w_mla_q_up[0], w_mla_kv_up[0],
                       gqa_out_norm_g[0], mla_out_norm_g[0], w_out[0], norm_ffn_g[0],
                       w_router_group[0], b_router_group[0], w_router_expert[0],
                       b_router_expert[0], w_expert_gate[0], w_expert_up[0],
                       w_expert_down[0], norm_final_g)
    meta_x = jnp.concatenate([meta_tokens.astype(F32),
                              jnp.zeros((META_PAD - N_META, D_MODEL), F32)], axis=0)
    mpos = jnp.arange(META_PAD, dtype=jnp.int32)
    meta_tables = _rope_tables(jnp.full((META_PAD,), -1, jnp.int32),
                               jnp.where(mpos < N_META, mpos, 0))
    _, ka_m, va_m, _, kb_m, vb_m = _project(meta_x, meta_tables, pw, META_PAD)
    bias = jnp.where(mpos < N_META, 0.0, NEG_BIG).astype(F32)[:, None]
    meta_kv = (ka_m, _with_ones(va_m.T, 1), kb_m, _with_ones(vb_m.T, MLA_HEADS))
    return (_trunk(x_prompt, pw, meta_kv, bias), _trunk(x_sample, pw, meta_kv, bias))
```
